```python
import math
import jax, jax.numpy as jnp
from jax import lax
import numpy as np

D_MODEL = 2048
BATCH = 4
SEQ = 2048
DEPTH = 2
DEC_BATCH = 128
DEC_SEQ = 8
PAST_LEN = 16384
PAGE_SIZE = 128

MIX_WIDTH = D_MODEL
N_MIXERS = 4
GROUP_WIDTH = MIX_WIDTH // N_MIXERS
POOL_WINDOWS = (2, 4, 8, 16)
POOL_GROUP = GROUP_WIDTH // len(POOL_WINDOWS)
POOL_BUF = max(POOL_WINDOWS) - 1
SG_CHUNK = 128
SG_HEADS = 4
SG_HEAD_DIM = GROUP_WIDTH // SG_HEADS
SSM_IN = 16
SSM_GROUPS = GROUP_WIDTH // SSM_IN
SSM_STATE = 64
CONV_WIDTH = 31
CONV_BUF = CONV_WIDTH - 1
D_FF = (11 * D_MODEL) // 4
IN_COLS = 6 * GROUP_WIDTH
SPLITS = tuple(GROUP_WIDTH * i for i in range(1, 6))
EPS = 1e-6
DT_MIN = 1e-3
DT_MAX = 1e-1

kernel_name = "hybrid_pool_gmlp_s5_conv_decode_step"


def rms_norm(x, g):
    xf = x.astype(jnp.float32)
    y = xf * lax.rsqrt(jnp.mean(xf * xf, axis=-1, keepdims=True) + EPS)
    return (y * g.astype(jnp.float32)).astype(x.dtype)


def group_rms_norm(x, g):
    N, L, _ = x.shape
    xf = x.astype(jnp.float32).reshape(N, L, N_MIXERS, GROUP_WIDTH)
    y = xf * lax.rsqrt(jnp.mean(xf * xf, axis=-1, keepdims=True) + EPS)
    return (y.reshape(N, L, MIX_WIDTH) * g.astype(jnp.float32)).astype(x.dtype)


def layer_norm(x, g, b):
    xf = x.astype(jnp.float32)
    mu = jnp.mean(xf, axis=-1, keepdims=True)
    var = jnp.mean(jnp.square(xf - mu), axis=-1, keepdims=True)
    y = (xf - mu) * lax.rsqrt(var + EPS)
    return (y * g.astype(jnp.float32) + b.astype(jnp.float32)).astype(x.dtype)


def swiglu(x, wg, wu, wd):
    return (jax.nn.silu(x @ wg) * (x @ wu)) @ wd


def pool_mixer(p, buf, start, w_grp, scale):
    N, L, _ = p.shape
    ext = jnp.concatenate([buf.astype(p.dtype), p], axis=1)
    cs = jnp.cumsum(ext.astype(jnp.float32), axis=1)
    cs = jnp.pad(cs, ((0, 0), (1, 0), (0, 0)))
    pos = start + jnp.arange(L)
    means = []
    for gi, w in enumerate(POOL_WINDOWS):
        lo, hi = gi * POOL_GROUP, (gi + 1) * POOL_GROUP
        s = cs[:, POOL_BUF + 1:POOL_BUF + 1 + L, lo:hi] - cs[:, POOL_BUF + 1 - w:POOL_BUF + 1 - w + L, lo:hi]
        cnt = jnp.minimum(w, pos + 1).astype(jnp.float32)
        means.append(s / cnt[None, :, None])
    mean = jnp.concatenate(means, axis=-1).astype(p.dtype)
    z = (mean - p).reshape(N, L, len(POOL_WINDOWS), POOL_GROUP)
    z = jnp.einsum('nlgc,gcd->nlgd', z, w_grp).reshape(N, L, GROUP_WIDTH)
    return z * scale, ext[:, -POOL_BUF:]


def spatial_gate(u, v, w_s, b_s):
    N, L, _ = u.shape
    mask = jnp.tril(jnp.ones((SG_CHUNK, SG_CHUNK), dtype=bool))
    w = jnp.where(mask[None], w_s, jnp.zeros_like(w_s))
    if L % SG_CHUNK == 0:
        nc = L // SG_CHUNK
        vc = v.reshape(N, nc, SG_CHUNK, SG_HEADS, SG_HEAD_DIM)
        mix = jnp.einsum('hts,ncshd->ncthd', w, vc) + b_s.T[None, None, :, :, None]
    else:
        vc = v.reshape(N, L, SG_HEADS, SG_HEAD_DIM)
        mix = jnp.einsum('hts,nshd->nthd', w[:, :L, :L], vc) + b_s[:, :L].T[None, :, :, None]
    return u * mix.reshape(N, L, GROUP_WIDTH)


def ssm_mixer(u, s_re, s_im, a_re, a_im, log_dt, b_re, b_im, c_re, c_im, d, w_glu, b_glu):
    f32 = jnp.float32
    N, L, _ = u.shape
    uf = u.astype(f32).reshape(N, L, SSM_GROUPS, SSM_IN)
    dt = jnp.exp(log_dt.astype(f32))[:, None]
    ar, ai = a_re.astype(f32), a_im.astype(f32)
    mag = jnp.exp(ar * dt)
    abar_re, abar_im = mag * jnp.cos(ai * dt), mag * jnp.sin(ai * dt)
    den = ar * ar + ai * ai
    nr, ni = abar_re - 1.0, abar_im
    coef_re = (nr * ar + ni * ai) / den
    coef_im = (ni * ar - nr * ai) / den
    br, bi = b_re.astype(f32), b_im.astype(f32)
    bbar_re = coef_re[..., None] * br - coef_im[..., None] * bi
    bbar_im = coef_re[..., None] * bi + coef_im[..., None] * br
    bu_re = jnp.einsum('nlgh,gph->nlgp', uf, bbar_re)
    bu_im = jnp.einsum('nlgh,gph->nlgp', uf, bbar_im)
    sr, si = s_re.astype(f32), s_im.astype(f32)
    bu_re = bu_re.at[:, 0].add(abar_re * sr - abar_im * si)
    bu_im = bu_im.at[:, 0].add(abar_re * si + abar_im * sr)
    shp = bu_re.shape
    elems = (jnp.broadcast_to(abar_re, shp), jnp.broadcast_to(abar_im, shp), bu_re, bu_im)

    def combine(e1, e2):
        a1r, a1i, b1r, b1i = e1
        a2r, a2i, b2r, b2i = e2
        return (a2r * a1r - a2i * a1i, a2r * a1i + a2i * a1r,
                a2r * b1r - a2i * b1i + b2r, a2r * b1i + a2i * b1r + b2i)

    _, _, x_re, x_im = lax.associative_scan(combine, elems, axis=1)
    y = (jnp.einsum('nlgp,ghp->nlgh', x_re, c_re.astype(f32))
         - jnp.einsum('nlgp,ghp->nlgh', x_im, c_im.astype(f32)))
    y = y.reshape(N, L, GROUP_WIDTH) + d.astype(f32) * u.astype(f32)
    g = jax.nn.gelu(y).astype(u.dtype)
    out = g * jax.nn.sigmoid(g @ w_glu + b_glu)
    return out, x_re[:, -1].astype(s_re.dtype), x_im[:, -1].astype(s_im.dtype)


def conv_mixer(a, gate, buf, w_dw, b_dw, ln_g, ln_b, w_pw):
    h = a * jax.nn.sigmoid(gate)
    ext = jnp.concatenate([buf.astype(h.dtype), h], axis=1)
    y = lax.conv_general_dilated(ext, w_dw[:, None, :], window_strides=(1,), padding='VALID',
                                 dimension_numbers=('NWC', 'WIO', 'NWC'),
                                 feature_group_count=GROUP_WIDTH) + b_dw
    y = layer_norm(y, ln_g, ln_b)
    y = jax.nn.silu(y) @ w_pw
    return y, ext[:, -CONV_BUF:]


def decoder_layer(x, start, pool_buf, conv_buf, s_re, s_im, lp):
    h = x + 0.5 * swiglu(rms_norm(x, lp['ffn1_norm']), lp['ffn1_w_gate'], lp['ffn1_w_up'], lp['ffn1_w_down'])
    n = rms_norm(h, lp['mix_norm'])
    proj = n @ lp['w_in']
    p_pool, sg_u, sg_v, p_ssm, conv_a, conv_g = jnp.split(proj, SPLITS, axis=-1)
    o_pool, new_pool = pool_mixer(p_pool, pool_buf, start, lp['pool_w'], lp['pool_scale'])
    o_sg = spatial_gate(sg_u, sg_v, lp['sg_w'], lp['sg_b'])
    o_ssm, new_re, new_im = ssm_mixer(p_ssm, s_re, s_im, lp['ssm_a_re'], lp['ssm_a_im'], lp['ssm_log_dt'],
                                      lp['ssm_b_re'], lp['ssm_b_im'], lp['ssm_c_re'], lp['ssm_c_im'],
                                      lp['ssm_d'], lp['ssm_w_glu'], lp['ssm_b_glu'])
    o_conv, new_conv = conv_mixer(conv_a, conv_g, conv_buf, lp['conv_w'], lp['conv_b'],
                                  lp['conv_ln_g'], lp['conv_ln_b'], lp['conv_w_pw'])
    mixed = group_rms_norm(jnp.concatenate([o_pool, o_sg, o_ssm, o_conv], axis=-1), lp['out_norm_g'])
    h = h + mixed @ lp['w_out']
    h = h + 0.5 * swiglu(rms_norm(h, lp['ffn2_norm']), lp['ffn2_w_gate'], lp['ffn2_w_up'], lp['ffn2_w_down'])
    return h, new_pool, new_conv, new_re, new_im, sg_v


def setup_inputs(seed: int = 0) -> dict:
    key = jax.random.key(seed)
    ks = iter(jax.random.split(key, 48))
    f32 = jnp.float32

    def nrm(shape, scale):
        return jax.random.normal(next(ks), shape, f32) * scale

    def gain(shape):
        return 1.0 + nrm(shape, 0.02)

    n_idx = jnp.arange(SSM_STATE, dtype=f32)
    inp = {}
    inp['x_prompt'] = nrm((BATCH, SEQ, D_MODEL), 1.0)
    inp['x_sample'] = nrm((DEC_BATCH, DEC_SEQ, D_MODEL), 1.0)
    inp['state_pool'] = nrm((DEPTH, DEC_BATCH, POOL_BUF, GROUP_WIDTH), 1.0)
    inp['state_conv'] = nrm((DEPTH, DEC_BATCH, CONV_BUF, GROUP_WIDTH), 0.5)
    inp['state_ssm_re'] = nrm((DEPTH, DEC_BATCH, SSM_GROUPS, SSM_STATE), 0.1)
    inp['state_ssm_im'] = nrm((DEPTH, DEC_BATCH, SSM_GROUPS, SSM_STATE), 0.1)
    inp['ffn1_norm'] = gain((DEPTH, D_MODEL))
    inp['ffn1_w_gate'] = nrm((DEPTH, D_MODEL, D_FF), D_MODEL ** -0.5)
    inp['ffn1_w_up'] = nrm((DEPTH, D_MODEL, D_FF), D_MODEL ** -0.5)
    inp['ffn1_w_down'] = nrm((DEPTH, D_FF, D_MODEL), D_FF ** -0.5)
    inp['mix_norm'] = gain((DEPTH, D_MODEL))
    inp['w_in'] = nrm((DEPTH, D_MODEL, IN_COLS), D_MODEL ** -0.5)
    inp['pool_w'] = nrm((DEPTH, len(POOL_WINDOWS), POOL_GROUP, POOL_GROUP), POOL_GROUP ** -0.5)
    inp['pool_scale'] = 1.0 + nrm((DEPTH, GROUP_WIDTH), 0.1)
    inp['sg_w'] = nrm((DEPTH, SG_HEADS, SG_CHUNK, SG_CHUNK), SG_CHUNK ** -0.5)
    inp['sg_b'] = 1.0 + nrm((DEPTH, SG_HEADS, SG_CHUNK), 0.1)
    inp['ssm_a_re'] = -0.5 + nrm((DEPTH, SSM_GROUPS, SSM_STATE), 0.01)
    inp['ssm_a_im'] = math.pi * n_idx + nrm((DEPTH, SSM_GROUPS, SSM_STATE), 0.01)
    inp['ssm_log_dt'] = jax.random.uniform(next(ks), (DEPTH, SSM_GROUPS), f32,
                                           math.log(DT_MIN), math.log(DT_MAX))
    inp['ssm_b_re'] = nrm((DEPTH, SSM_GROUPS, SSM_STATE, SSM_IN), (2 * SSM_IN) ** -0.5)
    inp['ssm_b_im'] = nrm((DEPTH, SSM_GROUPS, SSM_STATE, SSM_IN), (2 * SSM_IN) ** -0.5)
    inp['ssm_c_re'] = nrm((DEPTH, SSM_GROUPS, SSM_IN, SSM_STATE), (2 * SSM_STATE) ** -0.5)
    inp['ssm_c_im'] = nrm((DEPTH, SSM_GROUPS, SSM_IN, SSM_STATE), (2 * SSM_STATE) ** -0.5)
    inp['ssm_d'] = nrm((DEPTH, GROUP_WIDTH), 1.0)
    inp['ssm_w_glu'] = nrm((DEPTH, GROUP_WIDTH, GROUP_WIDTH), GROUP_WIDTH ** -0.5)
    inp['ssm_b_glu'] = nrm((DEPTH, GROUP_WIDTH), 0.02)
    inp['conv_w'] = nrm((DEPTH, CONV_WIDTH, GROUP_WIDTH), CONV_WIDTH ** -0.5)
    inp['conv_b'] = nrm((DEPTH, GROUP_WIDTH), 0.02)
    inp['conv_ln_g'] = gain((DEPTH, GROUP_WIDTH))
    inp['conv_ln_b'] = nrm((DEPTH, GROUP_WIDTH), 0.02)
    inp['conv_w_pw'] = nrm((DEPTH, GROUP_WIDTH, GROUP_WIDTH), GROUP_WIDTH ** -0.5)
    inp['out_norm_g'] = gain((DEPTH, MIX_WIDTH))
    inp['w_out'] = nrm((DEPTH, MIX_WIDTH, D_MODEL), MIX_WIDTH ** -0.5)
    inp['ffn2_norm'] = gain((DEPTH, D_MODEL))
    inp['ffn2_w_gate'] = nrm((DEPTH, D_MODEL, D_FF), D_MODEL ** -0.5)
    inp['ffn2_w_up'] = nrm((DEPTH, D_MODEL, D_FF), D_MODEL ** -0.5)
    inp['ffn2_w_down'] = nrm((DEPTH, D_FF, D_MODEL), D_FF ** -0.5)
    inp['final_norm'] = gain((D_MODEL,))
    return inp


def reference(x_prompt, x_sample, state_pool, state_conv, state_ssm_re, state_ssm_im,
              ffn1_norm, ffn1_w_gate, ffn1_w_up, ffn1_w_down, mix_norm, w_in,
              pool_w, pool_scale, sg_w, sg_b,
              ssm_a_re, ssm_a_im, ssm_log_dt, ssm_b_re, ssm_b_im, ssm_c_re, ssm_c_im,
              ssm_d, ssm_w_glu, ssm_b_glu,
              conv_w, conv_b, conv_ln_g, conv_ln_b, conv_w_pw,
              out_norm_g, w_out, ffn2_norm, ffn2_w_gate, ffn2_w_up, ffn2_w_down, final_norm):
    hp, hs = x_prompt, x_sample
    zero_pool = jnp.zeros((BATCH, POOL_BUF, GROUP_WIDTH), x_prompt.dtype)
    zero_conv = jnp.zeros((BATCH, CONV_BUF, GROUP_WIDTH), x_prompt.dtype)
    zero_ssm = jnp.zeros((BATCH, SSM_GROUPS, SSM_STATE), state_ssm_re.dtype)
    pool_p, pool_s, conv_p, conv_s = [], [], [], []
    re_p, im_p, re_s, im_s, v_s = [], [], [], [], []
    for l in range(DEPTH):
        lp = dict(ffn1_norm=ffn1_norm[l], ffn1_w_gate=ffn1_w_gate[l], ffn1_w_up=ffn1_w_up[l],
                  ffn1_w_down=ffn1_w_down[l], mix_norm=mix_norm[l], w_in=w_in[l],
                  pool_w=pool_w[l], pool_scale=pool_scale[l], sg_w=sg_w[l], sg_b=sg_b[l],
                  ssm_a_re=ssm_a_re[l], ssm_a_im=ssm_a_im[l], ssm_log_dt=ssm_log_dt[l],
                  ssm_b_re=ssm_b_re[l], ssm_b_im=ssm_b_im[l], ssm_c_re=ssm_c_re[l], ssm_c_im=ssm_c_im[l],
                  ssm_d=ssm_d[l], ssm_w_glu=ssm_w_glu[l], ssm_b_glu=ssm_b_glu[l],
                  conv_w=conv_w[l], conv_b=conv_b[l], conv_ln_g=conv_ln_g[l], conv_ln_b=conv_ln_b[l],
                  conv_w_pw=conv_w_pw[l], out_norm_g=out_norm_g[l], w_out=w_out[l],
                  ffn2_norm=ffn2_norm[l], ffn2_w_gate=ffn2_w_gate[l], ffn2_w_up=ffn2_w_up[l],
                  ffn2_w_down=ffn2_w_down[l])
        hp, npool, nconv, nre, nim, _ = decoder_layer(hp, 0, zero_pool, zero_conv, zero_ssm, zero_ssm, lp)
        pool_p.append(npool); conv_p.append(nconv); re_p.append(nre); im_p.append(nim)
        hs, npool, nconv, nre, nim, nv = decoder_layer(hs, PAST_LEN, state_pool[l], state_conv[l],
                                                       state_ssm_re[l], state_ssm_im[l], lp)
        pool_s.append(npool); conv_s.append(nconv); re_s.append(nre); im_s.append(nim); v_s.append(nv)
    y_prompt = rms_norm(hp, final_norm)
    y_sample = rms_norm(hs, final_norm)
    return (y_prompt, y_sample,
            jnp.stack(pool_p), jnp.stack(pool_s),
            jnp.stack(conv_p), jnp.stack(conv_s),
            jnp.stack(re_p), jnp.stack(im_p),
            jnp.stack(re_s), jnp.stack(im_s),
            jnp.stack(v_s))
```

```python
import functools

import jax
import jax.numpy as jnp
from jax import lax
from jax.experimental import pallas as pl
from jax.experimental.pallas import tpu as pltpu

F32 = jnp.float32
BF16 = jnp.bfloat16

EPS = 1e-6
GROUP_WIDTH = 512
N_MIXERS = 4
POOL_WINDOWS = (2, 4, 8, 16)
POOL_GROUP = GROUP_WIDTH // len(POOL_WINDOWS)
POOL_BUF = max(POOL_WINDOWS) - 1
SG_CHUNK = 128
SG_HEADS = 4
SSM_IN = 16
SSM_GROUPS = GROUP_WIDTH // SSM_IN
SSM_STATE = 64
SSM_LANES = SSM_GROUPS * SSM_STATE
SSM_BLOCKS = 2
SSM_BLOCK_IN = GROUP_WIDTH // SSM_BLOCKS
SSM_BLOCK_LANES = SSM_LANES // SSM_BLOCKS
CONV_WIDTH = 31
CONV_BUF = CONV_WIDTH - 1

SUBLANES = 8
LANES = 128
VMEM_LIMIT_BYTES = 56 * 1024 * 1024

POOL_PAD = 16
CONV_PAD = 32
SCAN_LANES = 512


def _cparams(sem):
    return pltpu.CompilerParams(dimension_semantics=sem, vmem_limit_bytes=VMEM_LIMIT_BYTES)


def _rms(x, g):
    return x * lax.rsqrt(jnp.mean(x * x, axis=-1, keepdims=True) + EPS) * g


def _dot(a, b):
    return jnp.dot(a, b, preferred_element_type=F32)


def _ffn_kernel(x_ref, g_ref, wg_ref, wu_ref, wd_ref, fg_ref, o_ref, xn_ref, *, final_norm):
    j = pl.program_id(1)

    @pl.when(j == 0)
    def _():
        xn_ref[...] = _rms(x_ref[...], g_ref[...]).astype(BF16)

    xn = xn_ref[...]
    g = _dot(xn, wg_ref[...])
    u = _dot(xn, wu_ref[...])
    a = (jax.nn.silu(g) * u).astype(BF16)
    d = _dot(a, wd_ref[...])

    @pl.when(j == 0)
    def _():
        o_ref[...] = d

    @pl.when(j > 0)
    def _():
        o_ref[...] += d

    @pl.when(j == pl.num_programs(1) - 1)
    def _():
        h = x_ref[...] + 0.5 * o_ref[...]
        if final_norm:
            h = _rms(h, fg_ref[...])
        o_ref[...] = h


def _ffn(x, norm_g, wg, wu, wd, final_g, *, final_norm, tm=512, tf=512):
    T, D = x.shape
    Fd = wg.shape[1]
    assert T % tm == 0 and Fd % tf == 0
    return pl.pallas_call(
        functools.partial(_ffn_kernel, final_norm=final_norm),
        grid=(T // tm, Fd // tf),
        in_specs=[
            pl.BlockSpec((tm, D), lambda i, j: (i, 0)),
            pl.BlockSpec((1, D), lambda i, j: (0, 0)),
            pl.BlockSpec((D, tf), lambda i, j: (0, j)),
            pl.BlockSpec((D, tf), lambda i, j: (0, j)),
            pl.BlockSpec((tf, D), lambda i, j: (j, 0)),
            pl.BlockSpec((1, D), lambda i, j: (0, 0)),
        ],
        out_specs=pl.BlockSpec((tm, D), lambda i, j: (i, 0)),
        out_shape=jax.ShapeDtypeStruct((T, D), F32),
        scratch_shapes=[pltpu.VMEM((tm, D), BF16)],
        compiler_params=_cparams(("parallel", "arbitrary")),
        name="ffn",
    )(x, norm_g.reshape(1, D), wg, wu, wd, final_g.reshape(1, D))


def _proj_in_kernel(h_ref, g_ref, w_ref, o_ref, xn_ref):
    @pl.when(pl.program_id(1) == 0)
    def _():
        xn_ref[...] = _rms(h_ref[...], g_ref[...]).astype(BF16)

    o_ref[...] = _dot(xn_ref[...], w_ref[...])


def _proj_in(h, norm_g, w, *, tm=512, tn=1024):
    T, D = h.shape
    N = w.shape[1]
    assert T % tm == 0 and N % tn == 0
    return pl.pallas_call(
        _proj_in_kernel,
        grid=(T // tm, N // tn),
        in_specs=[
            pl.BlockSpec((tm, D), lambda i, j: (i, 0)),
            pl.BlockSpec((1, D), lambda i, j: (0, 0)),
            pl.BlockSpec((D, tn), lambda i, j: (0, j)),
        ],
        out_specs=pl.BlockSpec((tm, tn), lambda i, j: (i, j)),
        out_shape=jax.ShapeDtypeStruct((T, N), F32),
        scratch_shapes=[pltpu.VMEM((tm, D), BF16)],
        compiler_params=_cparams(("parallel", "arbitrary")),
        name="proj_in",
    )(h, norm_g.reshape(1, D), w)


def _proj_out_kernel(h_ref, m_ref, w_ref, o_ref):
    o_ref[...] = h_ref[...] + _dot(m_ref[...], w_ref[...])


def _proj_out(h, mixed, w, *, tm=512, tn=1024):
    T, D = h.shape
    K = mixed.shape[1]
    assert T % tm == 0 and D % tn == 0
    return pl.pallas_call(
        _proj_out_kernel,
        grid=(T // tm, D // tn),
        in_specs=[
            pl.BlockSpec((tm, tn), lambda i, j: (i, j)),
            pl.BlockSpec((tm, K), lambda i, j: (i, 0)),
            pl.BlockSpec((K, tn), lambda i, j: (0, j)),
        ],
        out_specs=pl.BlockSpec((tm, tn), lambda i, j: (i, j)),
        out_shape=jax.ShapeDtypeStruct((T, D), F32),
        compiler_params=_cparams(("parallel", "arbitrary")),
        name="proj_out",
    )(h, mixed, w)


def _ssm_param_kernel(are_ref, aim_ref, ldt_ref, bre_ref, bim_ref,
                      abr_ref, abi_ref, bbr_ref, bbi_ref, ptr_ref, pti_ref, *, n_pow):
    ar, ai = are_ref[...], aim_ref[...]
    dt = jnp.exp(ldt_ref[...])
    mag = jnp.exp(ar * dt)
    abr, abi = mag * jnp.cos(ai * dt), mag * jnp.sin(ai * dt)
    den = ar * ar + ai * ai
    nr, ni = abr - 1.0, abi
    cre = (nr * ar + ni * ai) / den
    cim = (ni * ar - nr * ai) / den
    br, bi = bre_ref[...], bim_ref[...]
    bbr_ref[...] = cre * br - cim * bi
    bbi_ref[...] = cre * bi + cim * br
    abr_ref[...] = abr
    abi_ref[...] = abi
    ptr_ref[0:1, :] = abr
    pti_ref[0:1, :] = abi
    qr, qi = abr, abi
    n = 1
    while n < n_pow:
        tr, ti = ptr_ref[0:n, :], pti_ref[0:n, :]
        ptr_ref[n:2 * n, :] = tr * qr - ti * qi
        pti_ref[n:2 * n, :] = tr * qi + ti * qr
        qr, qi = qr * qr - qi * qi, 2.0 * qr * qi
        n *= 2


def _ssm_params(a_re, a_im, log_dt, b_re, b_im, *, n_pow):
    Lyr = a_re.shape[0]
    Q = SSM_LANES
    assert n_pow & (n_pow - 1) == 0
    are = a_re.reshape(Lyr, 1, Q)
    aim = a_im.reshape(Lyr, 1, Q)
    ldt = jnp.repeat(log_dt, SSM_STATE, axis=-1).reshape(Lyr, 1, Q)
    bre = b_re.reshape(Lyr, Q, SSM_IN).transpose(0, 2, 1)
    bim = b_im.reshape(Lyr, Q, SSM_IN).transpose(0, 2, 1)
    row = lambda r: pl.BlockSpec((None, r, Q), lambda l: (l, 0, 0))
    return pl.pallas_call(
        functools.partial(_ssm_param_kernel, n_pow=n_pow),
        grid=(Lyr,),
        in_specs=[row(1), row(1), row(1), row(SSM_IN), row(SSM_IN)],
        out_specs=[row(1), row(1), row(SSM_IN), row(SSM_IN), row(n_pow), row(n_pow)],
        out_shape=[jax.ShapeDtypeStruct((Lyr, 1, Q), F32), jax.ShapeDtypeStruct((Lyr, 1, Q), F32),
                   jax.ShapeDtypeStruct((Lyr, SSM_IN, Q), F32), jax.ShapeDtypeStruct((Lyr, SSM_IN, Q), F32),
                   jax.ShapeDtypeStruct((Lyr, n_pow, Q), F32), jax.ShapeDtypeStruct((Lyr, n_pow, Q), F32)],
        compiler_params=_cparams(("arbitrary",)),
        name="ssm_params",
    )(are, aim, ldt, bre, bim)


def _block_diag_b(bbt_re, bbt_im):
    gl = SSM_GROUPS // SSM_BLOCKS
    eye = jnp.eye(gl, dtype=F32)

    def one(bt):
        a = bt.reshape(SSM_IN, SSM_BLOCKS, gl, SSM_STATE).transpose(1, 2, 0, 3)
        return jnp.einsum("bghp,gk->bghkp", a, eye).reshape(SSM_BLOCKS, SSM_BLOCK_IN, SSM_BLOCK_LANES)

    return jnp.concatenate([one(bbt_re), one(bbt_im)], axis=-1).astype(BF16)


def _block_diag_c(c):
    gl = SSM_GROUPS // SSM_BLOCKS
    eye = jnp.eye(gl, dtype=F32)
    a = c.reshape(SSM_BLOCKS, gl, SSM_IN, SSM_STATE)
    return jnp.einsum("bghp,gk->bkpgh", a, eye).reshape(SSM_BLOCKS, SSM_BLOCK_LANES, SSM_BLOCK_IN).astype(BF16)


def _group_norm(o, g):
    return o * lax.rsqrt(jnp.mean(o * o, axis=-1, keepdims=True) + EPS) * g


def _conv_tail(y, lng, lnb, wpw):
    mu = jnp.mean(y, axis=-1, keepdims=True)
    yc = y - mu
    var = jnp.mean(yc * yc, axis=-1, keepdims=True)
    z = yc * lax.rsqrt(var + EPS) * lng + lnb
    return _dot(jax.nn.silu(z).astype(BF16), wpw)


def _ssm_tail(y, u, d, wglu, bglu):
    g = jax.nn.gelu(y + d * u)
    return g * jax.nn.sigmoid(_dot(g.astype(BF16), wglu) + bglu)


def _ssm_readout(xr_ref, xi_ref, cr_ref, ci_ref):
    ys = []
    for blk in range(SSM_BLOCKS):
        sl = slice(blk * SSM_BLOCK_LANES, (blk + 1) * SSM_BLOCK_LANES)
        ys.append(_dot(xr_ref[:, sl].astype(BF16), cr_ref[blk]) + _dot(xi_ref[:, sl].astype(BF16), ci_ref[blk]))
    return jnp.concatenate(ys, axis=-1)


def _ssm_drive(u, bblk_ref, xr_ref, xi_ref):
    ub = u.astype(BF16)
    for blk in range(SSM_BLOCKS):
        res = _dot(ub[:, blk * SSM_BLOCK_IN:(blk + 1) * SSM_BLOCK_IN], bblk_ref[blk])
        sl = slice(blk * SSM_BLOCK_LANES, (blk + 1) * SSM_BLOCK_LANES)
        xr_ref[:, sl] = res[:, :SSM_BLOCK_LANES]
        xi_ref[:, sl] = res[:, SSM_BLOCK_LANES:]


def _prompt_mixer_kernel(
        pp_ref, su_ref, sv_ref, ps_ref, ca_ref, cg_ref,
        poolw_ref, pscale_ref, sgw_ref, sgb_ref,
        abr_ref, abi_ref, ptr_ref, pti_ref, bblk_ref, cr_ref, ci_ref, ssmd_ref, wglu_ref, bglu_ref,
        convw_ref, convb_ref, lng_ref, lnb_ref, wpw_ref, ong_ref,
        mixed_ref, pool_out, conv_out, sre_out, sim_out,
        pext, cext, u3, ups, xr, xi, carr, cari, st_r, st_i, y3,
        *, tt):
    i = pl.program_id(1)
    tc = tt // SUBLANES
    gw = GROUP_WIDTH

    @pl.when(i == 0)
    def _():
        pext[0:POOL_PAD, :] = jnp.zeros((POOL_PAD, gw), F32)
        cext[0:CONV_PAD, :] = jnp.zeros((CONV_PAD, gw), F32)
        cext[tt + CONV_PAD:tt + CONV_PAD + SUBLANES, :] = jnp.zeros((SUBLANES, gw), F32)
        st_r[...] = jnp.zeros_like(st_r)
        st_i[...] = jnp.zeros_like(st_i)

    pext[POOL_PAD:POOL_PAD + tt, :] = pp_ref[...]
    rb = 64

    def pool_body(b, c):
        base = pl.multiple_of(b * rb, rb)
        n = rb + POOL_PAD
        e = pext[pl.ds(base, n), :]
        d = e
        sums = []
        for lvl in range(len(POOL_WINDOWS)):
            sh = 1 << lvl
            d = d[:, (POOL_GROUP if lvl > 0 else 0):]
            d = d + pltpu.roll(d, sh, axis=0)
            sums.append(d[POOL_PAD:, :POOL_GROUP])
        s = jnp.concatenate(sums, axis=-1)
        pos = lax.broadcasted_iota(jnp.int32, (rb, POOL_GROUP), 0) + (i * tt + base)
        cnt = jnp.concatenate([jnp.minimum(w, pos + 1).astype(F32) for w in POOL_WINDOWS], axis=-1)
        z = (s / cnt - e[POOL_PAD:, :]).astype(BF16)
        o = jnp.concatenate([_dot(z[:, g * POOL_GROUP:(g + 1) * POOL_GROUP], poolw_ref[g])
                             for g in range(len(POOL_WINDOWS))], axis=-1) * pscale_ref[...]
        mixed_ref[pl.ds(base, rb), 0:gw] = _group_norm(o, ong_ref[:, 0:gw]).astype(mixed_ref.dtype)
        return c

    lax.fori_loop(0, tt // rb, pool_body, 0)
    pool_out[0] = pext[tt + POOL_PAD - POOL_BUF:tt + POOL_PAD, :]
    pext[0:POOL_PAD, :] = pext[tt:tt + POOL_PAD, :]

    for c in range(tt // SG_CHUNK):
        rows = slice(c * SG_CHUNK, (c + 1) * SG_CHUNK)
        parts = []
        for h in range(SG_HEADS):
            cols = slice(h * SG_CHUNK, (h + 1) * SG_CHUNK)
            mix = _dot(sgw_ref[h], sv_ref[rows, cols].astype(BF16)) + sgb_ref[h]
            parts.append(su_ref[rows, cols] * mix)
        o = jnp.concatenate(parts, axis=-1)
        mixed_ref[rows, gw:2 * gw] = _group_norm(o, ong_ref[:, gw:2 * gw]).astype(mixed_ref.dtype)

    cext[CONV_PAD:CONV_PAD + tt, :] = ca_ref[...] * jax.nn.sigmoid(cg_ref[...])
    cb = 32
    off = CONV_PAD - CONV_BUF

    def conv_body(b, c):
        base = pl.multiple_of(b * cb, cb)
        nq = (off + CONV_WIDTH - 1) // SUBLANES + 1
        es = [cext[pl.ds(base + SUBLANES * q, cb + SUBLANES), :] for q in range(nq)]
        acc = jnp.zeros((cb, gw), F32) + convb_ref[...]
        for r in range(SUBLANES):
            a = None
            for q in range(nq):
                k = SUBLANES * q + r - off
                if 0 <= k < CONV_WIDTH:
                    t = convw_ref[k:k + 1, :] * es[q]
                    a = t if a is None else a + t
            if a is not None:
                acc = acc + a[r:r + cb, :]
        o = _conv_tail(acc, lng_ref[...], lnb_ref[...], wpw_ref[...])
        mixed_ref[pl.ds(base, cb), 3 * gw:4 * gw] = _group_norm(o, ong_ref[:, 3 * gw:4 * gw]).astype(mixed_ref.dtype)
        return c

    lax.fori_loop(0, tt // cb, conv_body, 0)
    conv_out[0] = cext[tt + CONV_PAD - CONV_BUF:tt + CONV_PAD, :]
    cext[0:CONV_PAD, :] = cext[tt:tt + CONV_PAD, :]

    u = ps_ref[...]
    nlb = gw // LANES
    for lb in range(nlb):
        u3[lb] = u[:, lb * LANES:(lb + 1) * LANES]
    for j in range(tc):
        ups[j * SUBLANES:(j + 1) * SUBLANES, :] = jnp.concatenate(
            [u3[lb, pl.ds(j, SUBLANES, stride=tc), :] for lb in range(nlb)], axis=-1)
    _ssm_drive(ups[...], bblk_ref, xr, xi)

    for lb in range(SSM_LANES // SCAN_LANES):
        ls = slice(lb * SCAN_LANES, (lb + 1) * SCAN_LANES)
        ar = jnp.broadcast_to(abr_ref[:, ls], (SUBLANES, SCAN_LANES))
        ai = jnp.broadcast_to(abi_ref[:, ls], (SUBLANES, SCAN_LANES))

        def scan_body(j, carry, ls=ls, ar=ar, ai=ai):
            sr, si = carry
            rows = pl.ds(pl.multiple_of(j * SUBLANES, SUBLANES), SUBLANES)
            nr = ar * sr - ai * si + xr[rows, ls]
            ni = ar * si + ai * sr + xi[rows, ls]
            xr[rows, ls] = nr
            xi[rows, ls] = ni
            return nr, ni

        zero = jnp.zeros((SUBLANES, SCAN_LANES), F32)
        lax.fori_loop(0, tc, scan_body, (zero, zero), unroll=4)

    car_r, car_i = st_r[...], st_i[...]
    at_r, at_i = ptr_ref[tc - 1:tc, :], pti_ref[tc - 1:tc, :]
    last = (tc - 1) * SUBLANES
    for c in range(SUBLANES):
        carr[c:c + 1, :] = car_r
        cari[c:c + 1, :] = car_i
        lf_r, lf_i = xr[last + c:last + c + 1, :], xi[last + c:last + c + 1, :]
        car_r, car_i = at_r * car_r - at_i * car_i + lf_r, at_r * car_i + at_i * car_r + lf_i
    st_r[...] = car_r
    st_i[...] = car_i
    sre_out[0] = car_r
    sim_out[0] = car_i

    for lb in range(SSM_LANES // SCAN_LANES):
        ls = slice(lb * SCAN_LANES, (lb + 1) * SCAN_LANES)
        c_r, c_i = carr[:, ls], cari[:, ls]

        def fix_body(jb, carry, ls=ls, c_r=c_r, c_i=c_i):
            j0 = pl.multiple_of(jb * SUBLANES, SUBLANES)
            tr, ti = ptr_ref[pl.ds(j0, SUBLANES), ls], pti_ref[pl.ds(j0, SUBLANES), ls]
            for jj in range(SUBLANES):
                rows = pl.ds(pl.multiple_of((j0 + jj) * SUBLANES, SUBLANES), SUBLANES)
                p_r, p_i = tr[jj:jj + 1, :], ti[jj:jj + 1, :]
                xr[rows, ls] = xr[rows, ls] + (p_r * c_r - p_i * c_i)
                xi[rows, ls] = xi[rows, ls] + (p_r * c_i + p_i * c_r)
            return carry

        lax.fori_loop(0, tc // SUBLANES, fix_body, 0)

    y = _ssm_readout(xr, xi, cr_ref, ci_ref)
    o = _ssm_tail(y, ups[...], ssmd_ref[...], wglu_ref[...], bglu_ref[...])
    on = _group_norm(o, ong_ref[:, 2 * gw:3 * gw])
    for j in range(tc):
        for lb in range(nlb):
            y3[lb, pl.ds(j, SUBLANES, stride=tc), :] = on[j * SUBLANES:(j + 1) * SUBLANES, lb * LANES:(lb + 1) * LANES]
    mixed_ref[:, 2 * gw:3 * gw] = jnp.concatenate([y3[lb] for lb in range(nlb)], axis=-1).astype(mixed_ref.dtype)


def _prompt_mixer(P, n_batch, seq, lw, *, tt):
    gw = GROUP_WIDTH
    nt = seq // tt
    tc = tt // SUBLANES
    assert seq % tt == 0 and tt % SG_CHUNK == 0 and tc % SUBLANES == 0 and lw["ptr"].shape[0] == tc
    col = lambda k: pl.BlockSpec((tt, gw), lambda b, i, k=k: (b * nt + i, k))
    full = lambda a: pl.BlockSpec(a.shape, lambda b, i, n=a.ndim: (0,) * n)
    weights = [lw[k] for k in ("poolw", "pscale", "sgw", "sgb", "abr", "abi", "ptr", "pti", "bblk", "cr", "ci",
                               "ssmd", "wglu", "bglu", "convw", "convb", "lng", "lnb", "wpw", "ong")]
    return pl.pallas_call(
        functools.partial(_prompt_mixer_kernel, tt=tt),
        grid=(n_batch, nt),
        in_specs=[col(k) for k in range(6)] + [full(w) for w in weights],
        out_specs=[
            pl.BlockSpec((tt, N_MIXERS * gw), lambda b, i: (b * nt + i, 0)),
            pl.BlockSpec((1, POOL_BUF, gw), lambda b, i: (b, 0, 0)),
            pl.BlockSpec((1, CONV_BUF, gw), lambda b, i: (b, 0, 0)),
            pl.BlockSpec((1, 1, SSM_LANES), lambda b, i: (b, 0, 0)),
            pl.BlockSpec((1, 1, SSM_LANES), lambda b, i: (b, 0, 0)),
        ],
        out_shape=[
            jax.ShapeDtypeStruct((n_batch * seq, N_MIXERS * gw), BF16),
            jax.ShapeDtypeStruct((n_batch, POOL_BUF, gw), F32),
            jax.ShapeDtypeStruct((n_batch, CONV_BUF, gw), F32),
            jax.ShapeDtypeStruct((n_batch, 1, SSM_LANES), F32),
            jax.ShapeDtypeStruct((n_batch, 1, SSM_LANES), F32),
        ],
        scratch_shapes=[
            pltpu.VMEM((tt + POOL_PAD, gw), F32),
            pltpu.VMEM((tt + CONV_PAD + SUBLANES, gw), F32),
            pltpu.VMEM((gw // LANES, tt, LANES), F32),
            pltpu.VMEM((tt, gw), F32),
            pltpu.VMEM((tt, SSM_LANES), F32),
            pltpu.VMEM((tt, SSM_LANES), F32),
            pltpu.VMEM((SUBLANES, SSM_LANES), F32),
            pltpu.VMEM((SUBLANES, SSM_LANES), F32),
            pltpu.VMEM((1, SSM_LANES), F32),
            pltpu.VMEM((1, SSM_LANES), F32),
            pltpu.VMEM((gw // LANES, tt, LANES), F32),
        ],
        compiler_params=_cparams(("parallel", "arbitrary")),
        name="prompt_mixer",
    )(P, P, P, P, P, P, *weights)


def _sample_mixer_kernel(
        pp_ref, su_ref, sv_ref, ps_ref, ca_ref, cg_ref,
        pbuf_ref, cbuf_ref, sre_ref, sim_ref,
        poolw_ref, pscale_ref, sgwv_ref, sgbv_ref,
        abr_ref, abi_ref, bblk_ref, cr_ref, ci_ref, ssmd_ref, wglu_ref, bglu_ref,
        convw_ref, convb_ref, lng_ref, lnb_ref, wpw_ref, ong_ref,
        mixed_ref, pool_out, conv_out, sre_out, sim_out,
        zs, hs, ys, us, xr, xi,
        *, start):
    steps, nb, gw = pp_ref.shape

    def put(dst_col, o):
        on = _group_norm(o, ong_ref[:, dst_col * gw:(dst_col + 1) * gw]).astype(mixed_ref.dtype)
        for t in range(steps):
            mixed_ref[t, :, dst_col * gw:(dst_col + 1) * gw] = on[t * nb:(t + 1) * nb, :]

    def pe(r):
        return pbuf_ref[r] if r < POOL_BUF else pp_ref[r - POOL_BUF]

    for t in range(steps):
        parts = []
        for g, w in enumerate(POOL_WINDOWS):
            cols = slice(g * POOL_GROUP, (g + 1) * POOL_GROUP)
            s = pe(POOL_BUF + t)[:, cols]
            for k in range(1, w):
                s = s + pe(POOL_BUF + t - k)[:, cols]
            parts.append(s / float(min(w, start + t + 1)))
        zs[t * nb:(t + 1) * nb, :] = jnp.concatenate(parts, axis=-1) - pp_ref[t]
    z = zs[...].astype(BF16)
    o = jnp.concatenate([_dot(z[:, g * POOL_GROUP:(g + 1) * POOL_GROUP], poolw_ref[g])
                         for g in range(len(POOL_WINDOWS))], axis=-1) * pscale_ref[...]
    put(0, o)
    for k in range(POOL_BUF):
        pool_out[k] = pe(steps + k)

    for t in range(steps):
        mix = jnp.zeros((nb, gw), F32) + sgbv_ref[t:t + 1, :]
        for s_ in range(t + 1):
            mix = mix + sgwv_ref[t, s_:s_ + 1, :] * sv_ref[s_]
        zs[t * nb:(t + 1) * nb, :] = su_ref[t] * mix
    put(1, zs[...])

    for t in range(steps):
        hs[t] = ca_ref[t] * jax.nn.sigmoid(cg_ref[t])

    def ce(r):
        return cbuf_ref[r] if r < CONV_BUF else hs[r - CONV_BUF]

    for t in range(steps):
        acc = jnp.zeros((nb, gw), F32) + convb_ref[...]
        for k in range(CONV_WIDTH):
            acc = acc + convw_ref[k:k + 1, :] * ce(t + k)
        ys[t * nb:(t + 1) * nb, :] = acc
    put(3, _conv_tail(ys[...], lng_ref[...], lnb_ref[...], wpw_ref[...]))
    for k in range(CONV_BUF):
        conv_out[k] = ce(steps + k)

    for t in range(steps):
        us[t * nb:(t + 1) * nb, :] = ps_ref[t]
    _ssm_drive(us[...], bblk_ref, xr, xi)
    lw = 256
    for lb in range(SSM_LANES // lw):
        ls = slice(lb * lw, (lb + 1) * lw)
        ar, ai = abr_ref[:, ls], abi_ref[:, ls]
        sr, si = sre_ref[:, ls], sim_ref[:, ls]
        for t in range(steps):
            rows = slice(t * nb, (t + 1) * nb)
            sr, si = ar * sr - ai * si + xr[rows, ls], ar * si + ai * sr + xi[rows, ls]
            xr[rows, ls] = sr
            xi[rows, ls] = si
        sre_out[:, ls] = sr
        sim_out[:, ls] = si
    y = _ssm_readout(xr, xi, cr_ref, ci_ref)
    put(2, _ssm_tail(y, us[...], ssmd_ref[...], wglu_ref[...], bglu_ref[...]))


def _sample_mixer(P3, row0, pbuf, cbuf, s_re, s_im, sw, *, start, nb=32):
    gw = GROUP_WIDTH
    steps = sw["sgwv"].shape[0]
    n_seq = P3.shape[1]
    assert n_seq % nb == 0 and row0 % steps == 0
    blk0 = row0 // steps
    col = lambda k: pl.BlockSpec((steps, nb, gw), lambda i, k=k: (blk0, i, k))
    full = lambda a: pl.BlockSpec(a.shape, lambda i, n=a.ndim: (0,) * n)
    slab = lambda r, c: pl.BlockSpec((r, nb, c), lambda i: (0, i, 0))
    rows = pl.BlockSpec((nb, SSM_LANES), lambda i: (i, 0))
    weights = [sw[k] for k in ("poolw", "pscale", "sgwv", "sgbv", "abr", "abi", "bblk", "cr", "ci",
                               "ssmd", "wglu", "bglu", "convw", "convb", "lng", "lnb", "wpw", "ong")]
    m = steps * nb
    return pl.pallas_call(
        functools.partial(_sample_mixer_kernel, start=start),
        grid=(n_seq // nb,),
        in_specs=[col(k) for k in range(6)] + [slab(POOL_BUF, gw), slab(CONV_BUF, gw), rows, rows]
        + [full(w) for w in weights],
        out_specs=[slab(steps, N_MIXERS * gw), slab(POOL_BUF, gw), slab(CONV_BUF, gw), rows, rows],
        out_shape=[
            jax.ShapeDtypeStruct((steps, n_seq, N_MIXERS * gw), BF16),
            jax.ShapeDtypeStruct((POOL_BUF, n_seq, gw), F32),
            jax.ShapeDtypeStruct((CONV_BUF, n_seq, gw), F32),
            jax.ShapeDtypeStruct((n_seq, SSM_LANES), F32),
            jax.ShapeDtypeStruct((n_seq, SSM_LANES), F32),
        ],
        scratch_shapes=[
            pltpu.VMEM((m, gw), F32),
            pltpu.VMEM((steps, nb, gw), F32),
            pltpu.VMEM((m, gw), F32),
            pltpu.VMEM((m, gw), F32),
            pltpu.VMEM((m, SSM_LANES), F32),
            pltpu.VMEM((m, SSM_LANES), F32),
        ],
        compiler_params=_cparams(("parallel",)),
        name="sample_mixer",
    )(P3, P3, P3, P3, P3, P3, pbuf, cbuf, s_re, s_im, *weights)


PROMPT_TILE = 512
PAST_LEN = 16384


def kernel(x_prompt, x_sample, state_pool, state_conv, state_ssm_re, state_ssm_im, ffn1_norm, ffn1_w_gate, ffn1_w_up, ffn1_w_down, mix_norm, w_in, pool_w, pool_scale, sg_w, sg_b, ssm_a_re, ssm_a_im, ssm_log_dt, ssm_b_re, ssm_b_im, ssm_c_re, ssm_c_im, ssm_d, ssm_w_glu, ssm_b_glu, conv_w, conv_b, conv_ln_g, conv_ln_b, conv_w_pw, out_norm_g, w_out, ffn2_norm, ffn2_w_gate, ffn2_w_up, ffn2_w_down, final_norm):
    return _forward(x_prompt, x_sample, state_pool, state_conv, state_ssm_re, state_ssm_im, ffn1_norm, ffn1_w_gate, ffn1_w_up, ffn1_w_down, mix_norm, w_in, pool_w, pool_scale, sg_w, sg_b, ssm_a_re, ssm_a_im, ssm_log_dt, ssm_b_re, ssm_b_im, ssm_c_re, ssm_c_im, ssm_d, ssm_w_glu, ssm_b_glu, conv_w, conv_b, conv_ln_g, conv_ln_b, conv_w_pw, out_norm_g, w_out, ffn2_norm, ffn2_w_gate, ffn2_w_up, ffn2_w_down, final_norm, prompt_tile=PROMPT_TILE, past_len=PAST_LEN)


def _forward(x_prompt, x_sample, state_pool, state_conv, state_ssm_re, state_ssm_im, ffn1_norm, ffn1_w_gate, ffn1_w_up, ffn1_w_down, mix_norm, w_in, pool_w, pool_scale, sg_w, sg_b, ssm_a_re, ssm_a_im, ssm_log_dt, ssm_b_re, ssm_b_im, ssm_c_re, ssm_c_im, ssm_d, ssm_w_glu, ssm_b_glu, conv_w, conv_b, conv_ln_g, conv_ln_b, conv_w_pw, out_norm_g, w_out, ffn2_norm, ffn2_w_gate, ffn2_w_up, ffn2_w_down, final_norm, *, prompt_tile, past_len):
    n_batch, seq, d_model = x_prompt.shape
    n_seq, steps, _ = x_sample.shape
    depth = w_in.shape[0]
    n_p = n_batch * seq
    gw = GROUP_WIDTH
    bf = lambda w: w.astype(BF16)
    row = lambda v: v.reshape(1, -1)

    x = jnp.concatenate([x_prompt.reshape(n_p, d_model),
                         x_sample.transpose(1, 0, 2).reshape(steps * n_seq, d_model)], axis=0)

    abr, abi, bbr, bbi, ptr, pti = _ssm_params(ssm_a_re, ssm_a_im, ssm_log_dt, ssm_b_re, ssm_b_im,
                                               n_pow=prompt_tile // SUBLANES)
    tril = jnp.tril(jnp.ones((SG_CHUNK, SG_CHUNK), dtype=bool))

    outs = {k: [] for k in ("pool_p", "pool_s", "conv_p", "conv_s", "re_p", "im_p", "re_s", "im_s", "v_s")}
    h = x
    for l in range(depth):
        sgw = jnp.where(tril[None], sg_w[l], 0.0)
        shared = dict(
            poolw=bf(pool_w[l]), pscale=row(pool_scale[l]), abr=abr[l], abi=abi[l],
            bblk=_block_diag_b(bbr[l], bbi[l]), cr=_block_diag_c(ssm_c_re[l]), ci=_block_diag_c(-ssm_c_im[l]),
            ssmd=row(ssm_d[l]), wglu=bf(ssm_w_glu[l]), bglu=row(ssm_b_glu[l]),
            convw=conv_w[l], convb=row(conv_b[l]), lng=row(conv_ln_g[l]), lnb=row(conv_ln_b[l]),
            wpw=bf(conv_w_pw[l]), ong=row(out_norm_g[l]))
        lw = dict(shared, sgw=bf(sgw), ptr=ptr[l], pti=pti[l],
                  sgb=jnp.broadcast_to(sg_b[l][:, :, None], (SG_HEADS, SG_CHUNK, SG_CHUNK)))
        sw = dict(shared,
                  sgwv=jnp.repeat(sgw[:, :steps, :steps].transpose(1, 2, 0), SG_CHUNK, axis=-1),
                  sgbv=jnp.repeat(sg_b[l][:, :steps].T, SG_CHUNK, axis=-1))

        h = _ffn(h, ffn1_norm[l], bf(ffn1_w_gate[l]), bf(ffn1_w_up[l]), bf(ffn1_w_down[l]), final_norm,
                 final_norm=False)
        proj = _proj_in(h, mix_norm[l], bf(w_in[l]))
        mixed_p, pool_p, conv_p, re_p, im_p = _prompt_mixer(proj, n_batch, seq, lw, tt=prompt_tile)
        mixed_s, pool_s, conv_s, re_s, im_s = _sample_mixer(
            proj.reshape(-1, n_seq, proj.shape[-1]), n_p // n_seq,
            state_pool[l].transpose(1, 0, 2), state_conv[l].transpose(1, 0, 2),
            state_ssm_re[l].reshape(n_seq, SSM_LANES), state_ssm_im[l].reshape(n_seq, SSM_LANES),
            sw, start=past_len)
        mixed = jnp.concatenate([mixed_p, mixed_s.reshape(steps * n_seq, N_MIXERS * gw)], axis=0)
        h = _proj_out(h, mixed, bf(w_out[l]))
        h = _ffn(h, ffn2_norm[l], bf(ffn2_w_gate[l]), bf(ffn2_w_up[l]), bf(ffn2_w_down[l]), final_norm,
                 final_norm=(l == depth - 1))

        st = lambda a: a.reshape(-1, SSM_GROUPS, SSM_STATE)
        outs["pool_p"].append(pool_p)
        outs["conv_p"].append(conv_p)
        outs["re_p"].append(st(re_p))
        outs["im_p"].append(st(im_p))
        outs["pool_s"].append(pool_s.transpose(1, 0, 2))
        outs["conv_s"].append(conv_s.transpose(1, 0, 2))
        outs["re_s"].append(st(re_s))
        outs["im_s"].append(st(im_s))
        outs["v_s"].append(proj[n_p:, 2 * gw:3 * gw].reshape(steps, n_seq, gw).transpose(1, 0, 2))

    y_prompt = h[:n_p].reshape(n_batch, seq, d_model)
    y_sample = h[n_p:].reshape(steps, n_seq, d_model).transpose(1, 0, 2)
    stk = lambda k: jnp.stack(outs[k])
    return (y_prompt, y_sample, stk("pool_p"), stk("pool_s"), stk("conv_p"), stk("conv_s"),
            stk("re_p"), stk("im_p"), stk("re_s"), stk("im_s"), stk("v_s"))
```

```python
import functools

import jax
import jax.numpy as jnp
from jax import lax
from jax.experimental import pallas as pl
from jax.experimental.pallas import tpu as pltpu

F32 = jnp.float32
BF16 = jnp.bfloat16

EPS = 1e-6
GROUP_WIDTH = 512
N_MIXERS = 4
POOL_WINDOWS = (2, 4, 8, 16)
POOL_GROUP = GROUP_WIDTH // len(POOL_WINDOWS)
POOL_BUF = max(POOL_WINDOWS) - 1
SG_CHUNK = 128
SG_HEADS = 4
SSM_IN = 16
SSM_GROUPS = GROUP_WIDTH // SSM_IN
SSM_STATE = 64
SSM_LANES = SSM_GROUPS * SSM_STATE
SSM_BLOCKS = 2
SSM_BLOCK_IN = GROUP_WIDTH // SSM_BLOCKS
SSM_BLOCK_LANES = SSM_LANES // SSM_BLOCKS
CONV_WIDTH = 31
CONV_BUF = CONV_WIDTH - 1

SUBLANES = 8
LANES = 128
VMEM_LIMIT_BYTES = 60 * 1024 * 1024

POOL_PAD = 16
CONV_PAD = 32
SCAN_LANES = 512


def _cparams(sem):
    return pltpu.CompilerParams(dimension_semantics=sem, vmem_limit_bytes=VMEM_LIMIT_BYTES)


def _rms(x, g):
    return x * lax.rsqrt(jnp.mean(x * x, axis=-1, keepdims=True) + EPS) * g


def _dot(a, b):
    return jnp.dot(a, b, preferred_element_type=F32)


def _ffn_kernel(x_ref, g_ref, wg_ref, wu_ref, wd_ref, fg_ref, o_ref, xn_ref, *, final_norm):
    j = pl.program_id(1)

    @pl.when(j == 0)
    def _():
        xn_ref[...] = _rms(x_ref[...], g_ref[...]).astype(BF16)
        o_ref[...] = jnp.zeros_like(o_ref)

    xn = xn_ref[...]
    g = _dot(xn, wg_ref[...].astype(BF16))
    u = _dot(xn, wu_ref[...].astype(BF16))
    a = (jax.nn.silu(g) * u).astype(BF16)
    o_ref[...] += _dot(a, wd_ref[...].astype(BF16))

    @pl.when(j == pl.num_programs(1) - 1)
    def _():
        h = x_ref[...] + 0.5 * o_ref[...]
        if final_norm:
            h = _rms(h, fg_ref[...])
        o_ref[...] = h


def _ffn(x, norm_g, wg, wu, wd, final_g, *, layer, final_norm, tm=1024, tf=256):
    T, D = x.shape
    Fd = wg.shape[-1]
    assert T % tm == 0 and Fd % tf == 0
    return pl.pallas_call(
        functools.partial(_ffn_kernel, final_norm=final_norm),
        grid=(T // tm, Fd // tf),
        in_specs=[
            pl.BlockSpec((tm, D), lambda i, j: (i, 0)),
            pl.BlockSpec((1, D), lambda i, j: (0, 0)),
            pl.BlockSpec((None, D, tf), lambda i, j: (layer, 0, j)),
            pl.BlockSpec((None, D, tf), lambda i, j: (layer, 0, j)),
            pl.BlockSpec((None, tf, D), lambda i, j: (layer, j, 0)),
            pl.BlockSpec((1, D), lambda i, j: (0, 0)),
        ],
        out_specs=pl.BlockSpec((tm, D), lambda i, j: (i, 0)),
        out_shape=jax.ShapeDtypeStruct((T, D), F32),
        scratch_shapes=[pltpu.VMEM((tm, D), BF16)],
        compiler_params=_cparams(("parallel", "arbitrary")),
        name="ffn",
    )(x, norm_g.reshape(1, D), wg, wu, wd, final_g.reshape(1, D))


def _proj_in_kernel(h_ref, g_ref, w_ref, o_ref, xn_ref):
    @pl.when(pl.program_id(1) == 0)
    def _():
        xn_ref[...] = _rms(h_ref[...], g_ref[...]).astype(BF16)

    o_ref[...] = _dot(xn_ref[...], w_ref[...].astype(BF16))


def _proj_in(h, norm_g, w, *, layer, tm=1024, tn=512):
    T, D = h.shape
    N = w.shape[-1]
    assert T % tm == 0 and N % tn == 0
    return pl.pallas_call(
        _proj_in_kernel,
        grid=(T // tm, N // tn),
        in_specs=[
            pl.BlockSpec((tm, D), lambda i, j: (i, 0)),
            pl.BlockSpec((1, D), lambda i, j: (0, 0)),
            pl.BlockSpec((None, D, tn), lambda i, j: (layer, 0, j)),
        ],
        out_specs=pl.BlockSpec((tm, tn), lambda i, j: (i, j)),
        out_shape=jax.ShapeDtypeStruct((T, N), F32),
        scratch_shapes=[pltpu.VMEM((tm, D), BF16)],
        compiler_params=_cparams(("parallel", "arbitrary")),
        name="proj_in",
    )(h, norm_g.reshape(1, D), w)


def _proj_out_kernel(h_ref, m_ref, w_ref, o_ref):
    o_ref[...] = h_ref[...] + _dot(m_ref[...], w_ref[...].astype(BF16))


def _proj_out(h, mixed, w, *, layer, tm=1024, tn=512):
    T, D = h.shape
    K = mixed.shape[1]
    assert T % tm == 0 and D % tn == 0
    return pl.pallas_call(
        _proj_out_kernel,
        grid=(T // tm, D // tn),
        in_specs=[
            pl.BlockSpec((tm, tn), lambda i, j: (i, j)),
            pl.BlockSpec((tm, K), lambda i, j: (i, 0)),
            pl.BlockSpec((None, K, tn), lambda i, j: (layer, 0, j)),
        ],
        out_specs=pl.BlockSpec((tm, tn), lambda i, j: (i, j)),
        out_shape=jax.ShapeDtypeStruct((T, D), F32),
        compiler_params=_cparams(("parallel", "arbitrary")),
        name="proj_out",
    )(h, mixed, w)


def _ssm_param_kernel(are_ref, aim_ref, ldt_ref, bre_ref, bim_ref,
                      abr_ref, abi_ref, bbr_ref, bbi_ref, ptr_ref, pti_ref, *, n_pow):
    ar, ai = are_ref[...], aim_ref[...]
    dt = jnp.exp(ldt_ref[...])
    mag = jnp.exp(ar * dt)
    abr, abi = mag * jnp.cos(ai * dt), mag * jnp.sin(ai * dt)
    den = ar * ar + ai * ai
    nr, ni = abr - 1.0, abi
    cre = (nr * ar + ni * ai) / den
    cim = (ni * ar - nr * ai) / den
    br, bi = bre_ref[...], bim_ref[...]
    bbr_ref[...] = cre * br - cim * bi
    bbi_ref[...] = cre * bi + cim * br
    abr_ref[...] = abr
    abi_ref[...] = abi
    ptr_ref[0:1, :] = abr
    pti_ref[0:1, :] = abi
    qr, qi = abr, abi
    n = 1
    while n < n_pow:
        tr, ti = ptr_ref[0:n, :], pti_ref[0:n, :]
        ptr_ref[n:2 * n, :] = tr * qr - ti * qi
        pti_ref[n:2 * n, :] = tr * qi + ti * qr
        qr, qi = qr * qr - qi * qi, 2.0 * qr * qi
        n *= 2


def _ssm_params(a_re, a_im, log_dt, b_re, b_im, *, n_pow):
    Lyr = a_re.shape[0]
    Q = SSM_LANES
    assert n_pow & (n_pow - 1) == 0
    are = a_re.reshape(Lyr, 1, Q)
    aim = a_im.reshape(Lyr, 1, Q)
    ldt = jnp.repeat(log_dt, SSM_STATE, axis=-1).reshape(Lyr, 1, Q)
    bre = b_re.reshape(Lyr, Q, SSM_IN).transpose(0, 2, 1)
    bim = b_im.reshape(Lyr, Q, SSM_IN).transpose(0, 2, 1)
    row = lambda r: pl.BlockSpec((None, r, Q), lambda l: (l, 0, 0))
    return pl.pallas_call(
        functools.partial(_ssm_param_kernel, n_pow=n_pow),
        grid=(Lyr,),
        in_specs=[row(1), row(1), row(1), row(SSM_IN), row(SSM_IN)],
        out_specs=[row(1), row(1), row(SSM_IN), row(SSM_IN), row(n_pow), row(n_pow)],
        out_shape=[jax.ShapeDtypeStruct((Lyr, 1, Q), F32), jax.ShapeDtypeStruct((Lyr, 1, Q), F32),
                   jax.ShapeDtypeStruct((Lyr, SSM_IN, Q), F32), jax.ShapeDtypeStruct((Lyr, SSM_IN, Q), F32),
                   jax.ShapeDtypeStruct((Lyr, n_pow, Q), F32), jax.ShapeDtypeStruct((Lyr, n_pow, Q), F32)],
        compiler_params=_cparams(("arbitrary",)),
        name="ssm_params",
    )(are, aim, ldt, bre, bim)


def _block_diag_b(bbt_re, bbt_im):
    gl = SSM_GROUPS // SSM_BLOCKS
    eye = jnp.eye(gl, dtype=F32)

    def one(bt):
        a = bt.reshape(SSM_IN, SSM_BLOCKS, gl, SSM_STATE).transpose(1, 2, 0, 3)
        return jnp.einsum("bghp,gk->bghkp", a, eye).reshape(SSM_BLOCKS, SSM_BLOCK_IN, SSM_BLOCK_LANES)

    return jnp.concatenate([one(bbt_re), one(bbt_im)], axis=-1).astype(BF16)


def _block_diag_c(c):
    gl = SSM_GROUPS // SSM_BLOCKS
    eye = jnp.eye(gl, dtype=F32)
    a = c.reshape(SSM_BLOCKS, gl, SSM_IN, SSM_STATE)
    return jnp.einsum("bghp,gk->bkpgh", a, eye).reshape(SSM_BLOCKS, SSM_BLOCK_LANES, SSM_BLOCK_IN).astype(BF16)


def _group_norm(o, g):
    return o * lax.rsqrt(jnp.mean(o * o, axis=-1, keepdims=True) + EPS) * g


def _conv_tail(y, lng, lnb, wpw):
    mu = jnp.mean(y, axis=-1, keepdims=True)
    yc = y - mu
    var = jnp.mean(yc * yc, axis=-1, keepdims=True)
    z = yc * lax.rsqrt(var + EPS) * lng + lnb
    return _dot(jax.nn.silu(z).astype(BF16), wpw)


def _ssm_tail(y, u, d, wglu, bglu):
    g = jax.nn.gelu(y + d * u)
    return g * jax.nn.sigmoid(_dot(g.astype(BF16), wglu) + bglu)


def _ssm_readout(xr_ref, xi_ref, cr_ref, ci_ref):
    ys = []
    for blk in range(SSM_BLOCKS):
        sl = slice(blk * SSM_BLOCK_LANES, (blk + 1) * SSM_BLOCK_LANES)
        ys.append(_dot(xr_ref[:, sl].astype(BF16), cr_ref[blk]) + _dot(xi_ref[:, sl].astype(BF16), ci_ref[blk]))
    return jnp.concatenate(ys, axis=-1)


def _ssm_drive(u, bblk_ref, xr_ref, xi_ref):
    ub = u.astype(BF16)
    for blk in range(SSM_BLOCKS):
        res = _dot(ub[:, blk * SSM_BLOCK_IN:(blk + 1) * SSM_BLOCK_IN], bblk_ref[blk])
        sl = slice(blk * SSM_BLOCK_LANES, (blk + 1) * SSM_BLOCK_LANES)
        xr_ref[:, sl] = res[:, :SSM_BLOCK_LANES]
        xi_ref[:, sl] = res[:, SSM_BLOCK_LANES:]


def _prompt_mixer_kernel(
        pp_ref, su_ref, sv_ref, ps_ref, ca_ref, cg_ref,
        poolw_ref, pscale_ref, sgw_ref, sgb_ref,
        abr_ref, abi_ref, ptr_ref, pti_ref, bblk_ref, cr_ref, ci_ref, ssmd_ref, wglu_ref, bglu_ref,
        convw_ref, convb_ref, lng_ref, lnb_ref, wpw_ref, ong_ref,
        mixed_ref, pool_out, conv_out, sre_out, sim_out,
        pext, cext, u3, ups, xr, xi, carr, cari, st_r, st_i, y3,
        *, tt):
    i = pl.program_id(1)
    tc = tt // SUBLANES
    gw = GROUP_WIDTH

    @pl.when(i == 0)
    def _():
        pext[0:POOL_PAD, :] = jnp.zeros((POOL_PAD, gw), F32)
        cext[0:CONV_PAD, :] = jnp.zeros((CONV_PAD, gw), F32)
        cext[tt + CONV_PAD:tt + CONV_PAD + SUBLANES, :] = jnp.zeros((SUBLANES, gw), F32)
        st_r[...] = jnp.zeros_like(st_r)
        st_i[...] = jnp.zeros_like(st_i)

    pext[POOL_PAD:POOL_PAD + tt, :] = pp_ref[...]
    rb = 64

    def pool_body(b, c):
        base = pl.multiple_of(b * rb, rb)
        n = rb + POOL_PAD
        e = pext[pl.ds(base, n), :]
        d = e
        sums = []
        for lvl in range(len(POOL_WINDOWS)):
            sh = 1 << lvl
            d = d[:, (POOL_GROUP if lvl > 0 else 0):]
            d = d + pltpu.roll(d, sh, axis=0)
            sums.append(d[POOL_PAD:, :POOL_GROUP])
        s = jnp.concatenate(sums, axis=-1)
        pos = lax.broadcasted_iota(jnp.int32, (rb, POOL_GROUP), 0) + (i * tt + base)
        cnt = jnp.concatenate([jnp.minimum(w, pos + 1).astype(F32) for w in POOL_WINDOWS], axis=-1)
        z = (s / cnt - e[POOL_PAD:, :]).astype(BF16)
        o = jnp.concatenate([_dot(z[:, g * POOL_GROUP:(g + 1) * POOL_GROUP], poolw_ref[g])
                             for g in range(len(POOL_WINDOWS))], axis=-1) * pscale_ref[...]
        mixed_ref[pl.ds(base, rb), 0:gw] = _group_norm(o, ong_ref[:, 0:gw]).astype(mixed_ref.dtype)
        return c

    lax.fori_loop(0, tt // rb, pool_body, 0)
    pool_out[0] = pext[tt + POOL_PAD - POOL_BUF:tt + POOL_PAD, :]
    pext[0:POOL_PAD, :] = pext[tt:tt + POOL_PAD, :]

    for c in range(tt // SG_CHUNK):
        rows = slice(c * SG_CHUNK, (c + 1) * SG_CHUNK)
        parts = []
        for h in range(SG_HEADS):
            cols = slice(h * SG_CHUNK, (h + 1) * SG_CHUNK)
            mix = _dot(sgw_ref[h], sv_ref[rows, cols].astype(BF16)) + sgb_ref[h]
            parts.append(su_ref[rows, cols] * mix)
        o = jnp.concatenate(parts, axis=-1)
        mixed_ref[rows, gw:2 * gw] = _group_norm(o, ong_ref[:, gw:2 * gw]).astype(mixed_ref.dtype)

    cext[CONV_PAD:CONV_PAD + tt, :] = ca_ref[...] * jax.nn.sigmoid(cg_ref[...])
    cb = 32
    off = CONV_PAD - CONV_BUF

    def conv_body(b, c):
        base = pl.multiple_of(b * cb, cb)
        nq = (off + CONV_WIDTH - 1) // SUBLANES + 1
        es = [cext[pl.ds(base + SUBLANES * q, cb + SUBLANES), :] for q in range(nq)]
        acc = jnp.zeros((cb, gw), F32) + convb_ref[...]
        for r in range(SUBLANES):
            a = None
            for q in range(nq):
                k = SUBLANES * q + r - off
                if 0 <= k < CONV_WIDTH:
                    t = convw_ref[k:k + 1, :] * es[q]
                    a = t if a is None else a + t
            if a is not None:
                acc = acc + a[r:r + cb, :]
        o = _conv_tail(acc, lng_ref[...], lnb_ref[...], wpw_ref[...])
        mixed_ref[pl.ds(base, cb), 3 * gw:4 * gw] = _group_norm(o, ong_ref[:, 3 * gw:4 * gw]).astype(mixed_ref.dtype)
        return c

    lax.fori_loop(0, tt // cb, conv_body, 0)
    conv_out[0] = cext[tt + CONV_PAD - CONV_BUF:tt + CONV_PAD, :]
    cext[0:CONV_PAD, :] = cext[tt:tt + CONV_PAD, :]

    u = ps_ref[...]
    nlb = gw // LANES
    for lb in range(nlb):
        u3[lb] = u[:, lb * LANES:(lb + 1) * LANES]
    for j in range(tc):
        ups[j * SUBLANES:(j + 1) * SUBLANES, :] = jnp.concatenate(
            [u3[lb, pl.ds(j, SUBLANES, stride=tc), :] for lb in range(nlb)], axis=-1)
    _ssm_drive(ups[...], bblk_ref, xr, xi)

    for lb in range(SSM_LANES // SCAN_LANES):
        ls = slice(lb * SCAN_LANES, (lb + 1) * SCAN_LANES)
        ar = jnp.broadcast_to(abr_ref[:, ls], (SUBLANES, SCAN_LANES))
        ai = jnp.broadcast_to(abi_ref[:, ls], (SUBLANES, SCAN_LANES))

        def scan_body(j, carry, ls=ls, ar=ar, ai=ai):
            sr, si = carry
            rows = pl.ds(pl.multiple_of(j * SUBLANES, SUBLANES), SUBLANES)
            nr = ar * sr - ai * si + xr[rows, ls]
            ni = ar * si + ai * sr + xi[rows, ls]
            xr[rows, ls] = nr
            xi[rows, ls] = ni
            return nr, ni

        zero = jnp.zeros((SUBLANES, SCAN_LANES), F32)
        lax.fori_loop(0, tc, scan_body, (zero, zero), unroll=4)

    car_r, car_i = st_r[...], st_i[...]
    at_r, at_i = ptr_ref[tc - 1:tc, :], pti_ref[tc - 1:tc, :]
    last = (tc - 1) * SUBLANES
    for c in range(SUBLANES):
        carr[c:c + 1, :] = car_r
        cari[c:c + 1, :] = car_i
        lf_r, lf_i = xr[last + c:last + c + 1, :], xi[last + c:last + c + 1, :]
        car_r, car_i = at_r * car_r - at_i * car_i + lf_r, at_r * car_i + at_i * car_r + lf_i
    st_r[...] = car_r
    st_i[...] = car_i
    sre_out[0] = car_r
    sim_out[0] = car_i

    for lb in range(SSM_LANES // SCAN_LANES):
        ls = slice(lb * SCAN_LANES, (lb + 1) * SCAN_LANES)
        c_r, c_i = carr[:, ls], cari[:, ls]

        def fix_body(jb, carry, ls=ls, c_r=c_r, c_i=c_i):
            j0 = pl.multiple_of(jb * SUBLANES, SUBLANES)
            tr, ti = ptr_ref[pl.ds(j0, SUBLANES), ls], pti_ref[pl.ds(j0, SUBLANES), ls]
            for jj in range(SUBLANES):
                rows = pl.ds(pl.multiple_of((j0 + jj) * SUBLANES, SUBLANES), SUBLANES)
                p_r, p_i = tr[jj:jj + 1, :], ti[jj:jj + 1, :]
                xr[rows, ls] = xr[rows, ls] + (p_r * c_r - p_i * c_i)
                xi[rows, ls] = xi[rows, ls] + (p_r * c_i + p_i * c_r)
            return carry

        lax.fori_loop(0, tc // SUBLANES, fix_body, 0)

    y = _ssm_readout(xr, xi, cr_ref, ci_ref)
    o = _ssm_tail(y, ups[...], ssmd_ref[...], wglu_ref[...], bglu_ref[...])
    on = _group_norm(o, ong_ref[:, 2 * gw:3 * gw])
    for j in range(tc):
        for lb in range(nlb):
            y3[lb, pl.ds(j, SUBLANES, stride=tc), :] = on[j * SUBLANES:(j + 1) * SUBLANES, lb * LANES:(lb + 1) * LANES]
    mixed_ref[:, 2 * gw:3 * gw] = jnp.concatenate([y3[lb] for lb in range(nlb)], axis=-1).astype(mixed_ref.dtype)


def _prompt_mixer(P, n_batch, seq, lw, *, tt):
    gw = GROUP_WIDTH
    nt = seq // tt
    tc = tt // SUBLANES
    assert seq % tt == 0 and tt % SG_CHUNK == 0 and tc % SUBLANES == 0 and lw["ptr"].shape[0] == tc
    col = lambda k: pl.BlockSpec((tt, gw), lambda b, i, k=k: (b * nt + i, k))
    full = lambda a: pl.BlockSpec(a.shape, lambda b, i, n=a.ndim: (0,) * n)
    weights = [lw[k] for k in ("poolw", "pscale", "sgw", "sgb", "abr", "abi", "ptr", "pti", "bblk", "cr", "ci",
                               "ssmd", "wglu", "bglu", "convw", "convb", "lng", "lnb", "wpw", "ong")]
    return pl.pallas_call(
        functools.partial(_prompt_mixer_kernel, tt=tt),
        grid=(n_batch, nt),
        in_specs=[col(k) for k in range(6)] + [full(w) for w in weights],
        out_specs=[
            pl.BlockSpec((tt, N_MIXERS * gw), lambda b, i: (b * nt + i, 0)),
            pl.BlockSpec((1, POOL_BUF, gw), lambda b, i: (b, 0, 0)),
            pl.BlockSpec((1, CONV_BUF, gw), lambda b, i: (b, 0, 0)),
            pl.BlockSpec((1, 1, SSM_LANES), lambda b, i: (b, 0, 0)),
            pl.BlockSpec((1, 1, SSM_LANES), lambda b, i: (b, 0, 0)),
        ],
        out_shape=[
            jax.ShapeDtypeStruct((n_batch * seq, N_MIXERS * gw), BF16),
            jax.ShapeDtypeStruct((n_batch, POOL_BUF, gw), F32),
            jax.ShapeDtypeStruct((n_batch, CONV_BUF, gw), F32),
            jax.ShapeDtypeStruct((n_batch, 1, SSM_LANES), F32),
            jax.ShapeDtypeStruct((n_batch, 1, SSM_LANES), F32),
        ],
        scratch_shapes=[
            pltpu.VMEM((tt + POOL_PAD, gw), F32),
            pltpu.VMEM((tt + CONV_PAD + SUBLANES, gw), F32),
            pltpu.VMEM((gw // LANES, tt, LANES), F32),
            pltpu.VMEM((tt, gw), F32),
            pltpu.VMEM((tt, SSM_LANES), F32),
            pltpu.VMEM((tt, SSM_LANES), F32),
            pltpu.VMEM((SUBLANES, SSM_LANES), F32),
            pltpu.VMEM((SUBLANES, SSM_LANES), F32),
            pltpu.VMEM((1, SSM_LANES), F32),
            pltpu.VMEM((1, SSM_LANES), F32),
            pltpu.VMEM((gw // LANES, tt, LANES), F32),
        ],
        compiler_params=_cparams(("parallel", "arbitrary")),
        name="prompt_mixer",
    )(P, P, P, P, P, P, *weights)


def _sample_mixer_kernel(
        pp_ref, su_ref, sv_ref, ps_ref, ca_ref, cg_ref,
        pbuf_ref, cbuf_ref, sre_ref, sim_ref,
        poolw_ref, pscale_ref, sgwv_ref, sgbv_ref,
        abr_ref, abi_ref, bblk_ref, cr_ref, ci_ref, ssmd_ref, wglu_ref, bglu_ref,
        convw_ref, convb_ref, lng_ref, lnb_ref, wpw_ref, ong_ref,
        mixed_ref, pool_out, conv_out, sre_out, sim_out,
        zs, hs, ys, us, xr, xi,
        *, start):
    steps, nb, gw = pp_ref.shape

    def put(dst_col, o):
        on = _group_norm(o, ong_ref[:, dst_col * gw:(dst_col + 1) * gw]).astype(mixed_ref.dtype)
        for t in range(steps):
            mixed_ref[t, :, dst_col * gw:(dst_col + 1) * gw] = on[t * nb:(t + 1) * nb, :]

    def pe(r):
        return pbuf_ref[r] if r < POOL_BUF else pp_ref[r - POOL_BUF]

    for t in range(steps):
        parts = []
        for g, w in enumerate(POOL_WINDOWS):
            cols = slice(g * POOL_GROUP, (g + 1) * POOL_GROUP)
            s = pe(POOL_BUF + t)[:, cols]
            for k in range(1, w):
                s = s + pe(POOL_BUF + t - k)[:, cols]
            parts.append(s / float(min(w, start + t + 1)))
        zs[t * nb:(t + 1) * nb, :] = jnp.concatenate(parts, axis=-1) - pp_ref[t]
    z = zs[...].astype(BF16)
    o = jnp.concatenate([_dot(z[:, g * POOL_GROUP:(g + 1) * POOL_GROUP], poolw_ref[g])
                         for g in range(len(POOL_WINDOWS))], axis=-1) * pscale_ref[...]
    put(0, o)
    for k in range(POOL_BUF):
        pool_out[k] = pe(steps + k)

    for t in range(steps):
        mix = jnp.zeros((nb, gw), F32) + sgbv_ref[t:t + 1, :]
        for s_ in range(t + 1):
            mix = mix + sgwv_ref[t, s_:s_ + 1, :] * sv_ref[s_]
        zs[t * nb:(t + 1) * nb, :] = su_ref[t] * mix
    put(1, zs[...])

    for t in range(steps):
        hs[t] = ca_ref[t] * jax.nn.sigmoid(cg_ref[t])

    def ce(r):
        return cbuf_ref[r] if r < CONV_BUF else hs[r - CONV_BUF]

    for t in range(steps):
        acc = jnp.zeros((nb, gw), F32) + convb_ref[...]
        for k in range(CONV_WIDTH):
            acc = acc + convw_ref[k:k + 1, :] * ce(t + k)
        ys[t * nb:(t + 1) * nb, :] = acc
    put(3, _conv_tail(ys[...], lng_ref[...], lnb_ref[...], wpw_ref[...]))
    for k in range(CONV_BUF):
        conv_out[k] = ce(steps + k)

    for t in range(steps):
        us[t * nb:(t + 1) * nb, :] = ps_ref[t]
    _ssm_drive(us[...], bblk_ref, xr, xi)
    lw = 256
    for lb in range(SSM_LANES // lw):
        ls = slice(lb * lw, (lb + 1) * lw)
        ar, ai = abr_ref[:, ls], abi_ref[:, ls]
        sr, si = sre_ref[:, ls], sim_ref[:, ls]
        for t in range(steps):
            rows = slice(t * nb, (t + 1) * nb)
            sr, si = ar * sr - ai * si + xr[rows, ls], ar * si + ai * sr + xi[rows, ls]
            xr[rows, ls] = sr
            xi[rows, ls] = si
        sre_out[:, ls] = sr
        sim_out[:, ls] = si
    y = _ssm_readout(xr, xi, cr_ref, ci_ref)
    put(2, _ssm_tail(y, us[...], ssmd_ref[...], wglu_ref[...], bglu_ref[...]))


def _sample_mixer(P3, row0, pbuf, cbuf, s_re, s_im, sw, *, start, nb=32):
    gw = GROUP_WIDTH
    steps = sw["sgwv"].shape[0]
    n_seq = P3.shape[1]
    assert n_seq % nb == 0 and row0 % steps == 0
    blk0 = row0 // steps
    col = lambda k: pl.BlockSpec((steps, nb, gw), lambda i, k=k: (blk0, i, k))
    full = lambda a: pl.BlockSpec(a.shape, lambda i, n=a.ndim: (0,) * n)
    slab = lambda r, c: pl.BlockSpec((r, nb, c), lambda i: (0, i, 0))
    rows = pl.BlockSpec((nb, SSM_LANES), lambda i: (i, 0))
    weights = [sw[k] for k in ("poolw", "pscale", "sgwv", "sgbv", "abr", "abi", "bblk", "cr", "ci",
                               "ssmd", "wglu", "bglu", "convw", "convb", "lng", "lnb", "wpw", "ong")]
    m = steps * nb
    return pl.pallas_call(
        functools.partial(_sample_mixer_kernel, start=start),
        grid=(n_seq // nb,),
        in_specs=[col(k) for k in range(6)] + [slab(POOL_BUF, gw), slab(CONV_BUF, gw), rows, rows]
        + [full(w) for w in weights],
        out_specs=[slab(steps, N_MIXERS * gw), slab(POOL_BUF, gw), slab(CONV_BUF, gw), rows, rows],
        out_shape=[
            jax.ShapeDtypeStruct((steps, n_seq, N_MIXERS * gw), BF16),
            jax.ShapeDtypeStruct((POOL_BUF, n_seq, gw), F32),
            jax.ShapeDtypeStruct((CONV_BUF, n_seq, gw), F32),
            jax.ShapeDtypeStruct((n_seq, SSM_LANES), F32),
            jax.ShapeDtypeStruct((n_seq, SSM_LANES), F32),
        ],
        scratch_shapes=[
            pltpu.VMEM((m, gw), F32),
            pltpu.VMEM((steps, nb, gw), F32),
            pltpu.VMEM((m, gw), F32),
            pltpu.VMEM((m, gw), F32),
            pltpu.VMEM((m, SSM_LANES), F32),
            pltpu.VMEM((m, SSM_LANES), F32),
        ],
        compiler_params=_cparams(("parallel",)),
        name="sample_mixer",
    )(P3, P3, P3, P3, P3, P3, pbuf, cbuf, s_re, s_im, *weights)


PROMPT_TILE = 512
PAST_LEN = 16384


def kernel(x_prompt, x_sample, state_pool, state_conv, state_ssm_re, state_ssm_im, ffn1_norm, ffn1_w_gate, ffn1_w_up, ffn1_w_down, mix_norm, w_in, pool_w, pool_scale, sg_w, sg_b, ssm_a_re, ssm_a_im, ssm_log_dt, ssm_b_re, ssm_b_im, ssm_c_re, ssm_c_im, ssm_d, ssm_w_glu, ssm_b_glu, conv_w, conv_b, conv_ln_g, conv_ln_b, conv_w_pw, out_norm_g, w_out, ffn2_norm, ffn2_w_gate, ffn2_w_up, ffn2_w_down, final_norm):
    return _forward(x_prompt, x_sample, state_pool, state_conv, state_ssm_re, state_ssm_im, ffn1_norm, ffn1_w_gate, ffn1_w_up, ffn1_w_down, mix_norm, w_in, pool_w, pool_scale, sg_w, sg_b, ssm_a_re, ssm_a_im, ssm_log_dt, ssm_b_re, ssm_b_im, ssm_c_re, ssm_c_im, ssm_d, ssm_w_glu, ssm_b_glu, conv_w, conv_b, conv_ln_g, conv_ln_b, conv_w_pw, out_norm_g, w_out, ffn2_norm, ffn2_w_gate, ffn2_w_up, ffn2_w_down, final_norm, prompt_tile=PROMPT_TILE, past_len=PAST_LEN)


def _forward(x_prompt, x_sample, state_pool, state_conv, state_ssm_re, state_ssm_im, ffn1_norm, ffn1_w_gate, ffn1_w_up, ffn1_w_down, mix_norm, w_in, pool_w, pool_scale, sg_w, sg_b, ssm_a_re, ssm_a_im, ssm_log_dt, ssm_b_re, ssm_b_im, ssm_c_re, ssm_c_im, ssm_d, ssm_w_glu, ssm_b_glu, conv_w, conv_b, conv_ln_g, conv_ln_b, conv_w_pw, out_norm_g, w_out, ffn2_norm, ffn2_w_gate, ffn2_w_up, ffn2_w_down, final_norm, *, prompt_tile, past_len):
    n_batch, seq, d_model = x_prompt.shape
    n_seq, steps, _ = x_sample.shape
    depth = w_in.shape[0]
    n_p = n_batch * seq
    gw = GROUP_WIDTH
    bf = lambda w: w.astype(BF16)
    row = lambda v: v.reshape(1, -1)

    x = jnp.concatenate([x_prompt.reshape(n_p, d_model),
                         x_sample.transpose(1, 0, 2).reshape(steps * n_seq, d_model)], axis=0)

    abr, abi, bbr, bbi, ptr, pti = _ssm_params(ssm_a_re, ssm_a_im, ssm_log_dt, ssm_b_re, ssm_b_im,
                                               n_pow=prompt_tile // SUBLANES)
    tril = jnp.tril(jnp.ones((SG_CHUNK, SG_CHUNK), dtype=bool))

    outs = {k: [] for k in ("pool_p", "pool_s", "conv_p", "conv_s", "re_p", "im_p", "re_s", "im_s", "v_s")}
    h = x
    for l in range(depth):
        sgw = jnp.where(tril[None], sg_w[l], 0.0)
        shared = dict(
            poolw=bf(pool_w[l]), pscale=row(pool_scale[l]), abr=abr[l], abi=abi[l],
            bblk=_block_diag_b(bbr[l], bbi[l]), cr=_block_diag_c(ssm_c_re[l]), ci=_block_diag_c(-ssm_c_im[l]),
            ssmd=row(ssm_d[l]), wglu=bf(ssm_w_glu[l]), bglu=row(ssm_b_glu[l]),
            convw=conv_w[l], convb=row(conv_b[l]), lng=row(conv_ln_g[l]), lnb=row(conv_ln_b[l]),
            wpw=bf(conv_w_pw[l]), ong=row(out_norm_g[l]))
        lw = dict(shared, sgw=bf(sgw), ptr=ptr[l], pti=pti[l],
                  sgb=jnp.broadcast_to(sg_b[l][:, :, None], (SG_HEADS, SG_CHUNK, SG_CHUNK)))
        sw = dict(shared,
                  sgwv=jnp.repeat(sgw[:, :steps, :steps].transpose(1, 2, 0), SG_CHUNK, axis=-1),
                  sgbv=jnp.repeat(sg_b[l][:, :steps].T, SG_CHUNK, axis=-1))

        h = _ffn(h, ffn1_norm[l], ffn1_w_gate, ffn1_w_up, ffn1_w_down, final_norm, layer=l, final_norm=False)
        proj = _proj_in(h, mix_norm[l], w_in, layer=l)
        mixed_p, pool_p, conv_p, re_p, im_p = _prompt_mixer(proj, n_batch, seq, lw, tt=prompt_tile)
        mixed_s, pool_s, conv_s, re_s, im_s = _sample_mixer(
            proj.reshape(-1, n_seq, proj.shape[-1]), n_p // n_seq,
            state_pool[l].transpose(1, 0, 2), state_conv[l].transpose(1, 0, 2),
            state_ssm_re[l].reshape(n_seq, SSM_LANES), state_ssm_im[l].reshape(n_seq, SSM_LANES),
            sw, start=past_len)
        mixed = jnp.concatenate([mixed_p, mixed_s.reshape(steps * n_seq, N_MIXERS * gw)], axis=0)
        h = _proj_out(h, mixed, w_out, layer=l)
        h = _ffn(h, ffn2_norm[l], ffn2_w_gate, ffn2_w_up, ffn2_w_down, final_norm, layer=l,
                 final_norm=(l == depth - 1))

        st = lambda a: a.reshape(-1, SSM_GROUPS, SSM_STATE)
        outs["pool_p"].append(pool_p)
        outs["conv_p"].append(conv_p)
        outs["re_p"].append(st(re_p))
        outs["im_p"].append(st(im_p))
        outs["pool_s"].append(pool_s.transpose(1, 0, 2))
        outs["conv_s"].append(conv_s.transpose(1, 0, 2))
        outs["re_s"].append(st(re_s))
        outs["im_s"].append(st(im_s))
        outs["v_s"].append(proj[n_p:, 2 * gw:3 * gw].reshape(steps, n_seq, gw).transpose(1, 0, 2))

    y_prompt = h[:n_p].reshape(n_batch, seq, d_model)
    y_sample = h[n_p:].reshape(steps, n_seq, d_model).transpose(1, 0, 2)
    stk = lambda k: jnp.stack(outs[k])
    return (y_prompt, y_sample, stk("pool_p"), stk("pool_s"), stk("conv_p"), stk("conv_s"),
            stk("re_p"), stk("im_p"), stk("re_s"), stk("im_s"), stk("v_s"))
```

```python
import functools
import math

import jax
import jax.numpy as jnp
from jax import lax
from jax.experimental import pallas as pl
from jax.experimental.pallas import tpu as pltpu

F32 = jnp.float32
BF16 = jnp.bfloat16

EPS = 1e-6
GROUP_WIDTH = 512
N_MIXERS = 4
POOL_WINDOWS = (2, 4, 8, 16)
POOL_GROUP = GROUP_WIDTH // len(POOL_WINDOWS)
POOL_BUF = max(POOL_WINDOWS) - 1
SG_CHUNK = 128
SG_HEADS = 4
SSM_IN = 16
SSM_GROUPS = GROUP_WIDTH // SSM_IN
SSM_STATE = 64
SSM_LANES = SSM_GROUPS * SSM_STATE
SSM_BLOCKS = 2
SSM_BLOCK_IN = GROUP_WIDTH // SSM_BLOCKS
SSM_BLOCK_LANES = SSM_LANES // SSM_BLOCKS
CONV_WIDTH = 31
CONV_BUF = CONV_WIDTH - 1

SUBLANES = 8
LANES = 128
VMEM_LIMIT_BYTES = 60 * 1024 * 1024

POOL_PAD = 16
CONV_PAD = 32
SCAN_LANES = 512
ROW_TILE = 1024
TAIL_ROWS = 256


def _cparams(sem):
    return pltpu.CompilerParams(dimension_semantics=sem, vmem_limit_bytes=VMEM_LIMIT_BYTES)


def _rms(x, g):
    return x * lax.rsqrt(jnp.mean(x * x, axis=-1, keepdims=True) + EPS) * g


def _dot(a, b):
    return jnp.dot(a, b, preferred_element_type=F32)


def _ffn_kernel(x_ref, g_ref, wg_ref, wu_ref, wd_ref, fg_ref, o_ref, xn_ref, *, final_norm):
    j = pl.program_id(1)

    @pl.when(j == 0)
    def _():
        xn_ref[...] = _rms(x_ref[...], g_ref[...]).astype(BF16)
        o_ref[...] = jnp.zeros_like(o_ref)

    xn = xn_ref[...]
    g = _dot(xn, wg_ref[...].astype(BF16))
    u = _dot(xn, wu_ref[...].astype(BF16))
    a = (jax.nn.silu(g) * u).astype(BF16)
    o_ref[...] += _dot(a, wd_ref[...].astype(BF16))

    @pl.when(j == pl.num_programs(1) - 1)
    def _():
        h = x_ref[...] + 0.5 * o_ref[...]
        if final_norm:
            h = _rms(h, fg_ref[...])
        o_ref[...] = h


def _ffn(x, norm_g, wg, wu, wd, final_g, *, layer, final_norm, tm=1024, tf=256):
    T, D = x.shape
    Fd = wg.shape[-1]
    assert T % tm == 0 and Fd % tf == 0
    return pl.pallas_call(
        functools.partial(_ffn_kernel, final_norm=final_norm),
        grid=(T // tm, Fd // tf),
        in_specs=[
            pl.BlockSpec((tm, D), lambda i, j: (i, 0)),
            pl.BlockSpec((1, D), lambda i, j: (0, 0)),
            pl.BlockSpec((None, D, tf), lambda i, j: (layer, 0, j)),
            pl.BlockSpec((None, D, tf), lambda i, j: (layer, 0, j)),
            pl.BlockSpec((None, tf, D), lambda i, j: (layer, j, 0)),
            pl.BlockSpec((1, D), lambda i, j: (0, 0)),
        ],
        out_specs=pl.BlockSpec((tm, D), lambda i, j: (i, 0)),
        out_shape=jax.ShapeDtypeStruct((T, D), F32),
        scratch_shapes=[pltpu.VMEM((tm, D), BF16)],
        compiler_params=_cparams(("parallel", "arbitrary")),
        name="ffn",
    )(x, norm_g.reshape(1, D), wg, wu, wd, final_g.reshape(1, D))


def _proj_in_kernel(h_ref, g_ref, w_ref, o_ref, xn_ref):
    @pl.when(pl.program_id(1) == 0)
    def _():
        xn_ref[...] = _rms(h_ref[...], g_ref[...]).astype(BF16)

    o_ref[...] = _dot(xn_ref[...], w_ref[...].astype(BF16))


def _proj_in(h, norm_g, w, *, layer, tm=1024, tn=512):
    T, D = h.shape
    N = w.shape[-1]
    assert T % tm == 0 and N % tn == 0
    return pl.pallas_call(
        _proj_in_kernel,
        grid=(T // tm, N // tn),
        in_specs=[
            pl.BlockSpec((tm, D), lambda i, j: (i, 0)),
            pl.BlockSpec((1, D), lambda i, j: (0, 0)),
            pl.BlockSpec((None, D, tn), lambda i, j: (layer, 0, j)),
        ],
        out_specs=pl.BlockSpec((tm, tn), lambda i, j: (i, j)),
        out_shape=jax.ShapeDtypeStruct((T, N), F32),
        scratch_shapes=[pltpu.VMEM((tm, D), BF16)],
        compiler_params=_cparams(("parallel", "arbitrary")),
        name="proj_in",
    )(h, norm_g.reshape(1, D), w)


def _proj_out_kernel(h_ref, mp_ref, ms_ref, w_ref, o_ref, *, n_prompt_tiles):
    i = pl.program_id(0)
    w = w_ref[...].astype(BF16)

    @pl.when(i < n_prompt_tiles)
    def _():
        o_ref[...] = h_ref[...] + _dot(mp_ref[...], w)

    @pl.when(i >= n_prompt_tiles)
    def _():
        o_ref[...] = h_ref[...] + _dot(ms_ref[...], w)


def _proj_out(h, mixed_p, mixed_s, w, *, layer, tm=1024, tn=512):
    T, D = h.shape
    K = mixed_p.shape[1]
    npt = mixed_p.shape[0] // tm
    assert T % tm == 0 and D % tn == 0 and mixed_p.shape[0] % tm == 0 and mixed_s.shape[0] % tm == 0
    assert mixed_p.shape[0] + mixed_s.shape[0] == T
    return pl.pallas_call(
        functools.partial(_proj_out_kernel, n_prompt_tiles=npt),
        grid=(T // tm, D // tn),
        in_specs=[
            pl.BlockSpec((tm, tn), lambda i, j: (i, j)),
            pl.BlockSpec((tm, K), lambda i, j: (jnp.minimum(i, npt - 1), 0)),
            pl.BlockSpec((tm, K), lambda i, j: (jnp.maximum(i - npt, 0), 0)),
            pl.BlockSpec((None, K, tn), lambda i, j: (layer, 0, j)),
        ],
        out_specs=pl.BlockSpec((tm, tn), lambda i, j: (i, j)),
        out_shape=jax.ShapeDtypeStruct((T, D), F32),
        compiler_params=_cparams(("arbitrary", "arbitrary")),
        name="proj_out",
    )(h, mixed_p, mixed_s, w)


def _ssm_param_kernel(are_ref, aim_ref, ldt_ref, bre_ref, bim_ref,
                      abr_ref, abi_ref, bbr_ref, bbi_ref, ptr_ref, pti_ref, *, n_pow):
    ar, ai = are_ref[...], aim_ref[...]
    dt = jnp.exp(ldt_ref[...])
    mag = jnp.exp(ar * dt)
    abr, abi = mag * jnp.cos(ai * dt), mag * jnp.sin(ai * dt)
    den = ar * ar + ai * ai
    nr, ni = abr - 1.0, abi
    cre = (nr * ar + ni * ai) / den
    cim = (ni * ar - nr * ai) / den
    br, bi = bre_ref[...], bim_ref[...]
    bbr_ref[...] = cre * br - cim * bi
    bbi_ref[...] = cre * bi + cim * br
    abr_ref[...] = abr
    abi_ref[...] = abi
    ptr_ref[0:1, :] = abr
    pti_ref[0:1, :] = abi
    qr, qi = abr, abi
    n = 1
    while n < n_pow:
        tr, ti = ptr_ref[0:n, :], pti_ref[0:n, :]
        ptr_ref[n:2 * n, :] = tr * qr - ti * qi
        pti_ref[n:2 * n, :] = tr * qi + ti * qr
        qr, qi = qr * qr - qi * qi, 2.0 * qr * qi
        n *= 2


def _ssm_params(a_re, a_im, log_dt, b_re, b_im, *, n_pow):
    Lyr = a_re.shape[0]
    Q = SSM_LANES
    assert n_pow & (n_pow - 1) == 0
    are = a_re.reshape(Lyr, 1, Q)
    aim = a_im.reshape(Lyr, 1, Q)
    ldt = jnp.repeat(log_dt, SSM_STATE, axis=-1).reshape(Lyr, 1, Q)
    bre = b_re.reshape(Lyr, Q, SSM_IN).transpose(0, 2, 1)
    bim = b_im.reshape(Lyr, Q, SSM_IN).transpose(0, 2, 1)
    row = lambda r: pl.BlockSpec((None, r, Q), lambda l: (l, 0, 0))
    return pl.pallas_call(
        functools.partial(_ssm_param_kernel, n_pow=n_pow),
        grid=(Lyr,),
        in_specs=[row(1), row(1), row(1), row(SSM_IN), row(SSM_IN)],
        out_specs=[row(1), row(1), row(SSM_IN), row(SSM_IN), row(n_pow), row(n_pow)],
        out_shape=[jax.ShapeDtypeStruct((Lyr, 1, Q), F32), jax.ShapeDtypeStruct((Lyr, 1, Q), F32),
                   jax.ShapeDtypeStruct((Lyr, SSM_IN, Q), F32), jax.ShapeDtypeStruct((Lyr, SSM_IN, Q), F32),
                   jax.ShapeDtypeStruct((Lyr, n_pow, Q), F32), jax.ShapeDtypeStruct((Lyr, n_pow, Q), F32)],
        compiler_params=_cparams(("arbitrary",)),
        name="ssm_params",
    )(are, aim, ldt, bre, bim)


def _block_diag_b(bbt_re, bbt_im):
    gl = SSM_GROUPS // SSM_BLOCKS
    eye = jnp.eye(gl, dtype=F32)

    def one(bt):
        a = bt.reshape(SSM_IN, SSM_BLOCKS, gl, SSM_STATE).transpose(1, 2, 0, 3)
        return jnp.einsum("bghp,gk->bghkp", a, eye).reshape(SSM_BLOCKS, SSM_BLOCK_IN, SSM_BLOCK_LANES)

    return jnp.concatenate([one(bbt_re), one(bbt_im)], axis=-1).astype(BF16)


def _block_diag_c(c):
    gl = SSM_GROUPS // SSM_BLOCKS
    eye = jnp.eye(gl, dtype=F32)
    a = c.reshape(SSM_BLOCKS, gl, SSM_IN, SSM_STATE)
    return jnp.einsum("bghp,gk->bkpgh", a, eye).reshape(SSM_BLOCKS, SSM_BLOCK_LANES, SSM_BLOCK_IN).astype(BF16)


def _group_norm(o, g):
    return o * lax.rsqrt(jnp.mean(o * o, axis=-1, keepdims=True) + EPS) * g


def _conv_tail(y, lng, lnb, wpw):
    mu = jnp.mean(y, axis=-1, keepdims=True)
    yc = y - mu
    var = jnp.mean(yc * yc, axis=-1, keepdims=True)
    z = yc * lax.rsqrt(var + EPS) * lng + lnb
    return _dot(jax.nn.silu(z).astype(BF16), wpw)


def _ssm_tail(y, u, d, wglu, bglu):
    g = jax.nn.gelu(y + d * u)
    return g * jax.nn.sigmoid(_dot(g.astype(BF16), wglu) + bglu)


def _ssm_readout(xr_ref, xi_ref, cr_ref, ci_ref):
    ys = []
    for blk in range(SSM_BLOCKS):
        sl = slice(blk * SSM_BLOCK_LANES, (blk + 1) * SSM_BLOCK_LANES)
        ys.append(_dot(xr_ref[:, sl].astype(BF16), cr_ref[blk]) + _dot(xi_ref[:, sl].astype(BF16), ci_ref[blk]))
    return jnp.concatenate(ys, axis=-1)


def _ssm_drive(u, bblk_ref, xr_ref, xi_ref):
    ub = u.astype(BF16)
    for blk in range(SSM_BLOCKS):
        res = _dot(ub[:, blk * SSM_BLOCK_IN:(blk + 1) * SSM_BLOCK_IN], bblk_ref[blk])
        sl = slice(blk * SSM_BLOCK_LANES, (blk + 1) * SSM_BLOCK_LANES)
        xr_ref[:, sl] = res[:, :SSM_BLOCK_LANES]
        xi_ref[:, sl] = res[:, SSM_BLOCK_LANES:]


def _prompt_mixer_kernel(
        pp_ref, su_ref, sv_ref, ps_ref, ca_ref, cg_ref,
        poolw_ref, pscale_ref, sgw_ref, sgb_ref,
        abr_ref, abi_ref, ptr_ref, pti_ref, bblk_ref, cr_ref, ci_ref, ssmd_ref, wglu_ref, bglu_ref,
        convw_ref, convb_ref, lng_ref, lnb_ref, wpw_ref, ong_ref,
        mixed_ref, pool_out, conv_out, sre_out, sim_out,
        pext, cext, csh, u3, ups, xr, xi, carr, cari, st_r, st_i, y3,
        *, tt):
    i = pl.program_id(1)
    tc = tt // SUBLANES
    gw = GROUP_WIDTH
    tail = min(TAIL_ROWS, tt)

    @pl.when(i == 0)
    def _():
        pext[0:POOL_PAD, :] = jnp.zeros((POOL_PAD, gw), F32)
        cext[0:CONV_PAD, :] = jnp.zeros((CONV_PAD, gw), F32)
        cext[tt + CONV_PAD:tt + CONV_PAD + SUBLANES, :] = jnp.zeros((SUBLANES, gw), F32)
        st_r[...] = jnp.zeros_like(st_r)
        st_i[...] = jnp.zeros_like(st_i)

    pext[POOL_PAD:POOL_PAD + tt, :] = pp_ref[...]
    rb = 64

    def pool_body(b, c):
        base = pl.multiple_of(b * rb, rb)
        n = rb + POOL_PAD
        e = pext[pl.ds(base, n), :]
        d = e
        sums = []
        for lvl in range(len(POOL_WINDOWS)):
            sh = 1 << lvl
            d = d[:, (POOL_GROUP if lvl > 0 else 0):]
            d = d + pltpu.roll(d, sh, axis=0)
            sums.append(d[POOL_PAD:, :POOL_GROUP])
        s = jnp.concatenate(sums, axis=-1)
        pos = lax.broadcasted_iota(jnp.int32, (rb, POOL_GROUP), 0) + (i * tt + base)
        cnt = jnp.concatenate([jnp.minimum(w, pos + 1).astype(F32) for w in POOL_WINDOWS], axis=-1)
        ups[pl.ds(base, rb), :] = s / cnt - e[POOL_PAD:, :]
        return c

    lax.fori_loop(0, tt // rb, pool_body, 0)
    for ch in range(tt // tail):
        rows = slice(ch * tail, (ch + 1) * tail)
        z = ups[rows, :].astype(BF16)
        o = jnp.concatenate([_dot(z[:, g * POOL_GROUP:(g + 1) * POOL_GROUP], poolw_ref[g])
                             for g in range(len(POOL_WINDOWS))], axis=-1) * pscale_ref[...]
        mixed_ref[rows, 0:gw] = _group_norm(o, ong_ref[:, 0:gw]).astype(mixed_ref.dtype)
    pool_out[0] = pext[tt + POOL_PAD - POOL_BUF:tt + POOL_PAD, :]
    pext[0:POOL_PAD, :] = pext[tt:tt + POOL_PAD, :]

    for c in range(tt // SG_CHUNK):
        rows = slice(c * SG_CHUNK, (c + 1) * SG_CHUNK)
        parts = []
        for h in range(SG_HEADS):
            cols = slice(h * SG_CHUNK, (h + 1) * SG_CHUNK)
            mix = _dot(sgw_ref[h], sv_ref[rows, cols].astype(BF16)) + sgb_ref[h]
            parts.append(su_ref[rows, cols] * mix)
        o = jnp.concatenate(parts, axis=-1)
        mixed_ref[rows, gw:2 * gw] = _group_norm(o, ong_ref[:, gw:2 * gw]).astype(mixed_ref.dtype)

    cext[CONV_PAD:CONV_PAD + tt, :] = ca_ref[...] * jax.nn.sigmoid(cg_ref[...])
    cb = 32
    off = CONV_PAD - CONV_BUF

    for r in range(1, SUBLANES):
        csh[r - 1] = cext[r:r + tt + CONV_PAD, :]

    def conv_body(b, c):
        base = pl.multiple_of(b * cb, cb)
        acc = jnp.zeros((cb, gw), F32) + convb_ref[...]
        for k in range(CONV_WIDTH):
            q, r = divmod(k + off, SUBLANES)
            rows = pl.ds(base + SUBLANES * q, cb)
            tap = cext[rows, :] if r == 0 else csh[r - 1, rows, :]
            acc = acc + convw_ref[k:k + 1, :] * tap
        ups[pl.ds(base, cb), :] = acc
        return c

    lax.fori_loop(0, tt // cb, conv_body, 0)
    for ch in range(tt // tail):
        rows = slice(ch * tail, (ch + 1) * tail)
        o = _conv_tail(ups[rows, :], lng_ref[...], lnb_ref[...], wpw_ref[...])
        mixed_ref[rows, 3 * gw:4 * gw] = _group_norm(o, ong_ref[:, 3 * gw:4 * gw]).astype(mixed_ref.dtype)
    conv_out[0] = cext[tt + CONV_PAD - CONV_BUF:tt + CONV_PAD, :]
    cext[0:CONV_PAD, :] = cext[tt:tt + CONV_PAD, :]

    u = ps_ref[...]
    nlb = gw // LANES
    for lb in range(nlb):
        u3[lb] = u[:, lb * LANES:(lb + 1) * LANES]
    for j in range(tc):
        ups[j * SUBLANES:(j + 1) * SUBLANES, :] = jnp.concatenate(
            [u3[lb, pl.ds(j, SUBLANES, stride=tc), :] for lb in range(nlb)], axis=-1)
    _ssm_drive(ups[...], bblk_ref, xr, xi)

    for lb in range(SSM_LANES // SCAN_LANES):
        ls = slice(lb * SCAN_LANES, (lb + 1) * SCAN_LANES)
        ar = jnp.broadcast_to(abr_ref[:, ls], (SUBLANES, SCAN_LANES))
        ai = jnp.broadcast_to(abi_ref[:, ls], (SUBLANES, SCAN_LANES))

        def scan_body(j, carry, ls=ls, ar=ar, ai=ai):
            sr, si = carry
            rows = pl.ds(pl.multiple_of(j * SUBLANES, SUBLANES), SUBLANES)
            nr = ar * sr - ai * si + xr[rows, ls]
            ni = ar * si + ai * sr + xi[rows, ls]
            xr[rows, ls] = nr
            xi[rows, ls] = ni
            return nr, ni

        zero = jnp.zeros((SUBLANES, SCAN_LANES), F32)
        lax.fori_loop(0, tc, scan_body, (zero, zero), unroll=4)

    car_r, car_i = st_r[...], st_i[...]
    at_r, at_i = ptr_ref[tc - 1:tc, :], pti_ref[tc - 1:tc, :]
    last = (tc - 1) * SUBLANES
    for c in range(SUBLANES):
        carr[c:c + 1, :] = car_r
        cari[c:c + 1, :] = car_i
        lf_r, lf_i = xr[last + c:last + c + 1, :], xi[last + c:last + c + 1, :]
        car_r, car_i = at_r * car_r - at_i * car_i + lf_r, at_r * car_i + at_i * car_r + lf_i
    st_r[...] = car_r
    st_i[...] = car_i
    sre_out[0] = car_r
    sim_out[0] = car_i

    for lb in range(SSM_LANES // SCAN_LANES):
        ls = slice(lb * SCAN_LANES, (lb + 1) * SCAN_LANES)
        c_r, c_i = carr[:, ls], cari[:, ls]

        def fix_body(jb, carry, ls=ls, c_r=c_r, c_i=c_i):
            j0 = pl.multiple_of(jb * SUBLANES, SUBLANES)
            tr, ti = ptr_ref[pl.ds(j0, SUBLANES), ls], pti_ref[pl.ds(j0, SUBLANES), ls]
            for jj in range(SUBLANES):
                rows = pl.ds(pl.multiple_of((j0 + jj) * SUBLANES, SUBLANES), SUBLANES)
                p_r, p_i = tr[jj:jj + 1, :], ti[jj:jj + 1, :]
                xr[rows, ls] = xr[rows, ls] + (p_r * c_r - p_i * c_i)
                xi[rows, ls] = xi[rows, ls] + (p_r * c_i + p_i * c_r)
            return carry

        lax.fori_loop(0, tc // SUBLANES, fix_body, 0)

    y = _ssm_readout(xr, xi, cr_ref, ci_ref)
    o = _ssm_tail(y, ups[...], ssmd_ref[...], wglu_ref[...], bglu_ref[...])
    on = _group_norm(o, ong_ref[:, 2 * gw:3 * gw])
    for j in range(tc):
        for lb in range(nlb):
            y3[lb, pl.ds(j, SUBLANES, stride=tc), :] = on[j * SUBLANES:(j + 1) * SUBLANES, lb * LANES:(lb + 1) * LANES]
    mixed_ref[:, 2 * gw:3 * gw] = jnp.concatenate([y3[lb] for lb in range(nlb)], axis=-1).astype(mixed_ref.dtype)


def _prompt_mixer(P, n_batch, seq, lw, *, tt):
    gw = GROUP_WIDTH
    nt = seq // tt
    tc = tt // SUBLANES
    assert seq % tt == 0 and tt % SG_CHUNK == 0 and tc % SUBLANES == 0 and lw["ptr"].shape[0] == tc
    col = lambda k: pl.BlockSpec((tt, gw), lambda b, i, k=k: (b * nt + i, k))
    full = lambda a: pl.BlockSpec(a.shape, lambda b, i, n=a.ndim: (0,) * n)
    weights = [lw[k] for k in ("poolw", "pscale", "sgw", "sgb", "abr", "abi", "ptr", "pti", "bblk", "cr", "ci",
                               "ssmd", "wglu", "bglu", "convw", "convb", "lng", "lnb", "wpw", "ong")]
    return pl.pallas_call(
        functools.partial(_prompt_mixer_kernel, tt=tt),
        grid=(n_batch, nt),
        in_specs=[col(k) for k in range(6)] + [full(w) for w in weights],
        out_specs=[
            pl.BlockSpec((tt, N_MIXERS * gw), lambda b, i: (b * nt + i, 0)),
            pl.BlockSpec((1, POOL_BUF, gw), lambda b, i: (b, 0, 0)),
            pl.BlockSpec((1, CONV_BUF, gw), lambda b, i: (b, 0, 0)),
            pl.BlockSpec((1, 1, SSM_LANES), lambda b, i: (b, 0, 0)),
            pl.BlockSpec((1, 1, SSM_LANES), lambda b, i: (b, 0, 0)),
        ],
        out_shape=[
            jax.ShapeDtypeStruct((n_batch * seq, N_MIXERS * gw), BF16),
            jax.ShapeDtypeStruct((n_batch, POOL_BUF, gw), F32),
            jax.ShapeDtypeStruct((n_batch, CONV_BUF, gw), F32),
            jax.ShapeDtypeStruct((n_batch, 1, SSM_LANES), F32),
            jax.ShapeDtypeStruct((n_batch, 1, SSM_LANES), F32),
        ],
        scratch_shapes=[
            pltpu.VMEM((tt + POOL_PAD, gw), F32),
            pltpu.VMEM((tt + CONV_PAD + SUBLANES, gw), F32),
            pltpu.VMEM((SUBLANES - 1, tt + CONV_PAD, gw), F32),
            pltpu.VMEM((gw // LANES, tt, LANES), F32),
            pltpu.VMEM((tt, gw), F32),
            pltpu.VMEM((tt, SSM_LANES), F32),
            pltpu.VMEM((tt, SSM_LANES), F32),
            pltpu.VMEM((SUBLANES, SSM_LANES), F32),
            pltpu.VMEM((SUBLANES, SSM_LANES), F32),
            pltpu.VMEM((1, SSM_LANES), F32),
            pltpu.VMEM((1, SSM_LANES), F32),
            pltpu.VMEM((gw // LANES, tt, LANES), F32),
        ],
        compiler_params=_cparams(("parallel", "arbitrary")),
        name="prompt_mixer",
    )(P, P, P, P, P, P, *weights)


def _sample_mixer_kernel(
        pp_ref, su_ref, sv_ref, ps_ref, ca_ref, cg_ref,
        pbuf_ref, cbuf_ref, sre_ref, sim_ref,
        poolw_ref, pscale_ref, sgwv_ref, sgbv_ref,
        abr_ref, abi_ref, bblk_ref, cr_ref, ci_ref, ssmd_ref, wglu_ref, bglu_ref,
        convw_ref, convb_ref, lng_ref, lnb_ref, wpw_ref, ong_ref,
        mixed_ref, pool_out, conv_out, sre_out, sim_out,
        zs, hs, ys, us, xr, xi,
        *, start):
    steps, nb, gw = pp_ref.shape

    def put(dst_col, o):
        on = _group_norm(o, ong_ref[:, dst_col * gw:(dst_col + 1) * gw]).astype(mixed_ref.dtype)
        for t in range(steps):
            mixed_ref[t, :, dst_col * gw:(dst_col + 1) * gw] = on[t * nb:(t + 1) * nb, :]

    def pe(r):
        return pbuf_ref[r] if r < POOL_BUF else pp_ref[r - POOL_BUF]

    for t in range(steps):
        parts = []
        for g, w in enumerate(POOL_WINDOWS):
            cols = slice(g * POOL_GROUP, (g + 1) * POOL_GROUP)
            s = pe(POOL_BUF + t)[:, cols]
            for k in range(1, w):
                s = s + pe(POOL_BUF + t - k)[:, cols]
            parts.append(s / float(min(w, start + t + 1)))
        zs[t * nb:(t + 1) * nb, :] = jnp.concatenate(parts, axis=-1) - pp_ref[t]
    z = zs[...].astype(BF16)
    o = jnp.concatenate([_dot(z[:, g * POOL_GROUP:(g + 1) * POOL_GROUP], poolw_ref[g])
                         for g in range(len(POOL_WINDOWS))], axis=-1) * pscale_ref[...]
    put(0, o)
    for k in range(POOL_BUF):
        pool_out[k] = pe(steps + k)

    for t in range(steps):
        mix = jnp.zeros((nb, gw), F32) + sgbv_ref[t:t + 1, :]
        for s_ in range(t + 1):
            mix = mix + sgwv_ref[t, s_:s_ + 1, :] * sv_ref[s_]
        zs[t * nb:(t + 1) * nb, :] = su_ref[t] * mix
    put(1, zs[...])

    for t in range(steps):
        hs[t] = ca_ref[t] * jax.nn.sigmoid(cg_ref[t])

    def ce(r):
        return cbuf_ref[r] if r < CONV_BUF else hs[r - CONV_BUF]

    for t in range(steps):
        acc = jnp.zeros((nb, gw), F32) + convb_ref[...]
        for k in range(CONV_WIDTH):
            acc = acc + convw_ref[k:k + 1, :] * ce(t + k)
        ys[t * nb:(t + 1) * nb, :] = acc
    put(3, _conv_tail(ys[...], lng_ref[...], lnb_ref[...], wpw_ref[...]))
    for k in range(CONV_BUF):
        conv_out[k] = ce(steps + k)

    for t in range(steps):
        us[t * nb:(t + 1) * nb, :] = ps_ref[t]
    _ssm_drive(us[...], bblk_ref, xr, xi)
    lw = 256
    for lb in range(SSM_LANES // lw):
        ls = slice(lb * lw, (lb + 1) * lw)
        ar, ai = abr_ref[:, ls], abi_ref[:, ls]
        sr, si = sre_ref[:, ls], sim_ref[:, ls]
        for t in range(steps):
            rows = slice(t * nb, (t + 1) * nb)
            sr, si = ar * sr - ai * si + xr[rows, ls], ar * si + ai * sr + xi[rows, ls]
            xr[rows, ls] = sr
            xi[rows, ls] = si
        sre_out[:, ls] = sr
        sim_out[:, ls] = si
    y = _ssm_readout(xr, xi, cr_ref, ci_ref)
    put(2, _ssm_tail(y, us[...], ssmd_ref[...], wglu_ref[...], bglu_ref[...]))


def _sample_mixer(P3, row0, pbuf, cbuf, s_re, s_im, sw, *, start, nb=32):
    gw = GROUP_WIDTH
    steps = sw["sgwv"].shape[0]
    n_seq = P3.shape[1]
    assert n_seq % nb == 0 and row0 % steps == 0
    blk0 = row0 // steps
    col = lambda k: pl.BlockSpec((steps, nb, gw), lambda i, k=k: (blk0, i, k))
    full = lambda a: pl.BlockSpec(a.shape, lambda i, n=a.ndim: (0,) * n)
    slab = lambda r, c: pl.BlockSpec((r, nb, c), lambda i: (0, i, 0))
    rows = pl.BlockSpec((nb, SSM_LANES), lambda i: (i, 0))
    weights = [sw[k] for k in ("poolw", "pscale", "sgwv", "sgbv", "abr", "abi", "bblk", "cr", "ci",
                               "ssmd", "wglu", "bglu", "convw", "convb", "lng", "lnb", "wpw", "ong")]
    m = steps * nb
    return pl.pallas_call(
        functools.partial(_sample_mixer_kernel, start=start),
        grid=(n_seq // nb,),
        in_specs=[col(k) for k in range(6)] + [slab(POOL_BUF, gw), slab(CONV_BUF, gw), rows, rows]
        + [full(w) for w in weights],
        out_specs=[slab(steps, N_MIXERS * gw), slab(POOL_BUF, gw), slab(CONV_BUF, gw), rows, rows],
        out_shape=[
            jax.ShapeDtypeStruct((steps, n_seq, N_MIXERS * gw), BF16),
            jax.ShapeDtypeStruct((POOL_BUF, n_seq, gw), F32),
            jax.ShapeDtypeStruct((CONV_BUF, n_seq, gw), F32),
            jax.ShapeDtypeStruct((n_seq, SSM_LANES), F32),
            jax.ShapeDtypeStruct((n_seq, SSM_LANES), F32),
        ],
        scratch_shapes=[
            pltpu.VMEM((m, gw), F32),
            pltpu.VMEM((steps, nb, gw), F32),
            pltpu.VMEM((m, gw), F32),
            pltpu.VMEM((m, gw), F32),
            pltpu.VMEM((m, SSM_LANES), F32),
            pltpu.VMEM((m, SSM_LANES), F32),
        ],
        compiler_params=_cparams(("parallel",)),
        name="sample_mixer",
    )(P3, P3, P3, P3, P3, P3, pbuf, cbuf, s_re, s_im, *weights)


PROMPT_TILE = 512
PAST_LEN = 16384


def kernel(x_prompt, x_sample, state_pool, state_conv, state_ssm_re, state_ssm_im, ffn1_norm, ffn1_w_gate, ffn1_w_up, ffn1_w_down, mix_norm, w_in, pool_w, pool_scale, sg_w, sg_b, ssm_a_re, ssm_a_im, ssm_log_dt, ssm_b_re, ssm_b_im, ssm_c_re, ssm_c_im, ssm_d, ssm_w_glu, ssm_b_glu, conv_w, conv_b, conv_ln_g, conv_ln_b, conv_w_pw, out_norm_g, w_out, ffn2_norm, ffn2_w_gate, ffn2_w_up, ffn2_w_down, final_norm):
    return _forward(x_prompt, x_sample, state_pool, state_conv, state_ssm_re, state_ssm_im, ffn1_norm, ffn1_w_gate, ffn1_w_up, ffn1_w_down, mix_norm, w_in, pool_w, pool_scale, sg_w, sg_b, ssm_a_re, ssm_a_im, ssm_log_dt, ssm_b_re, ssm_b_im, ssm_c_re, ssm_c_im, ssm_d, ssm_w_glu, ssm_b_glu, conv_w, conv_b, conv_ln_g, conv_ln_b, conv_w_pw, out_norm_g, w_out, ffn2_norm, ffn2_w_gate, ffn2_w_up, ffn2_w_down, final_norm, prompt_tile=PROMPT_TILE, past_len=PAST_LEN)


def _forward(x_prompt, x_sample, state_pool, state_conv, state_ssm_re, state_ssm_im, ffn1_norm, ffn1_w_gate, ffn1_w_up, ffn1_w_down, mix_norm, w_in, pool_w, pool_scale, sg_w, sg_b, ssm_a_re, ssm_a_im, ssm_log_dt, ssm_b_re, ssm_b_im, ssm_c_re, ssm_c_im, ssm_d, ssm_w_glu, ssm_b_glu, conv_w, conv_b, conv_ln_g, conv_ln_b, conv_w_pw, out_norm_g, w_out, ffn2_norm, ffn2_w_gate, ffn2_w_up, ffn2_w_down, final_norm, *, prompt_tile, past_len):
    n_batch, seq, d_model = x_prompt.shape
    n_seq, steps, _ = x_sample.shape
    depth = w_in.shape[0]
    n_p = n_batch * seq
    tm = math.gcd(math.gcd(n_p, steps * n_seq), ROW_TILE)
    gw = GROUP_WIDTH
    bf = lambda w: w.astype(BF16)
    row = lambda v: v.reshape(1, -1)

    x = jnp.concatenate([x_prompt.reshape(n_p, d_model),
                         x_sample.transpose(1, 0, 2).reshape(steps * n_seq, d_model)], axis=0)

    abr, abi, bbr, bbi, ptr, pti = _ssm_params(ssm_a_re, ssm_a_im, ssm_log_dt, ssm_b_re, ssm_b_im,
                                               n_pow=prompt_tile // SUBLANES)
    tril = jnp.tril(jnp.ones((SG_CHUNK, SG_CHUNK), dtype=bool))

    outs = {k: [] for k in ("pool_p", "pool_s", "conv_p", "conv_s", "re_p", "im_p", "re_s", "im_s", "v_s")}
    h = x
    for l in range(depth):
        sgw = jnp.where(tril[None], sg_w[l], 0.0)
        shared = dict(
            poolw=bf(pool_w[l]), pscale=row(pool_scale[l]), abr=abr[l], abi=abi[l],
            bblk=_block_diag_b(bbr[l], bbi[l]), cr=_block_diag_c(ssm_c_re[l]), ci=_block_diag_c(-ssm_c_im[l]),
            ssmd=row(ssm_d[l]), wglu=bf(ssm_w_glu[l]), bglu=row(ssm_b_glu[l]),
            convw=conv_w[l], convb=row(conv_b[l]), lng=row(conv_ln_g[l]), lnb=row(conv_ln_b[l]),
            wpw=bf(conv_w_pw[l]), ong=row(out_norm_g[l]))
        lw = dict(shared, sgw=bf(sgw), ptr=ptr[l], pti=pti[l],
                  sgb=jnp.broadcast_to(sg_b[l][:, :, None], (SG_HEADS, SG_CHUNK, SG_CHUNK)))
        sw = dict(shared,
                  sgwv=jnp.repeat(sgw[:, :steps, :steps].transpose(1, 2, 0), SG_CHUNK, axis=-1),
                  sgbv=jnp.repeat(sg_b[l][:, :steps].T, SG_CHUNK, axis=-1))

        h = _ffn(h, ffn1_norm[l], ffn1_w_gate, ffn1_w_up, ffn1_w_down, final_norm, layer=l, final_norm=False, tm=tm)
        proj = _proj_in(h, mix_norm[l], w_in, layer=l, tm=tm)
        mixed_p, pool_p, conv_p, re_p, im_p = _prompt_mixer(proj, n_batch, seq, lw, tt=prompt_tile)
        mixed_s, pool_s, conv_s, re_s, im_s = _sample_mixer(
            proj.reshape(-1, n_seq, proj.shape[-1]), n_p // n_seq,
            state_pool[l].transpose(1, 0, 2), state_conv[l].transpose(1, 0, 2),
            state_ssm_re[l].reshape(n_seq, SSM_LANES), state_ssm_im[l].reshape(n_seq, SSM_LANES),
            sw, start=past_len)
        h = _proj_out(h, mixed_p, mixed_s.reshape(steps * n_seq, N_MIXERS * gw), w_out, layer=l, tm=tm)
        h = _ffn(h, ffn2_norm[l], ffn2_w_gate, ffn2_w_up, ffn2_w_down, final_norm, layer=l,
                 final_norm=(l == depth - 1), tm=tm)

        st = lambda a: a.reshape(-1, SSM_GROUPS, SSM_STATE)
        outs["pool_p"].append(pool_p)
        outs["conv_p"].append(conv_p)
        outs["re_p"].append(st(re_p))
        outs["im_p"].append(st(im_p))
        outs["pool_s"].append(pool_s.transpose(1, 0, 2))
        outs["conv_s"].append(conv_s.transpose(1, 0, 2))
        outs["re_s"].append(st(re_s))
        outs["im_s"].append(st(im_s))
        outs["v_s"].append(proj[n_p:, 2 * gw:3 * gw].reshape(steps, n_seq, gw).transpose(1, 0, 2))

    y_prompt = h[:n_p].reshape(n_batch, seq, d_model)
    y_sample = h[n_p:].reshape(steps, n_seq, d_model).transpose(1, 0, 2)
    stk = lambda k: jnp.stack(outs[k])
    return (y_prompt, y_sample, stk("pool_p"), stk("pool_s"), stk("conv_p"), stk("conv_s"),
            stk("re_p"), stk("im_p"), stk("re_s"), stk("im_s"), stk("v_s"))
```

```python
import functools
import math

import jax
import jax.numpy as jnp
from jax import lax
from jax.experimental import pallas as pl
from jax.experimental.pallas import tpu as pltpu

F32 = jnp.float32
BF16 = jnp.bfloat16

EPS = 1e-6
GROUP_WIDTH = 512
N_MIXERS = 4
POOL_WINDOWS = (2, 4, 8, 16)
POOL_GROUP = GROUP_WIDTH // len(POOL_WINDOWS)
POOL_BUF = max(POOL_WINDOWS) - 1
SG_CHUNK = 128
SG_HEADS = 4
SSM_IN = 16
SSM_GROUPS = GROUP_WIDTH // SSM_IN
SSM_STATE = 64
SSM_LANES = SSM_GROUPS * SSM_STATE
SSM_BLOCKS = 2
SSM_BLOCK_IN = GROUP_WIDTH // SSM_BLOCKS
SSM_BLOCK_LANES = SSM_LANES // SSM_BLOCKS
CONV_WIDTH = 31
CONV_BUF = CONV_WIDTH - 1

SUBLANES = 8
LANES = 128
VMEM_LIMIT_BYTES = 60 * 1024 * 1024

POOL_PAD = 16
CONV_PAD = 32
SCAN_LANES = 512
ROW_TILE = 1024
TAIL_ROWS = 256


def _cparams(sem):
    return pltpu.CompilerParams(dimension_semantics=sem, vmem_limit_bytes=VMEM_LIMIT_BYTES)


def _rms(x, g):
    return x * lax.rsqrt(jnp.mean(x * x, axis=-1, keepdims=True) + EPS) * g


def _dot(a, b):
    return jnp.dot(a, b, preferred_element_type=F32)


def _ffn_kernel(x_ref, g_ref, wg_ref, wu_ref, wd_ref, fg_ref, o_ref, xn_ref, *, final_norm):
    j = pl.program_id(1)

    @pl.when(j == 0)
    def _():
        xn_ref[...] = _rms(x_ref[...], g_ref[...]).astype(BF16)
        o_ref[...] = jnp.zeros_like(o_ref)

    xn = xn_ref[...]
    g = _dot(xn, wg_ref[...].astype(BF16))
    u = _dot(xn, wu_ref[...].astype(BF16))
    a = (jax.nn.silu(g) * u).astype(BF16)
    o_ref[...] += _dot(a, wd_ref[...].astype(BF16))

    @pl.when(j == pl.num_programs(1) - 1)
    def _():
        h = x_ref[...] + 0.5 * o_ref[...]
        if final_norm:
            h = _rms(h, fg_ref[...])
        o_ref[...] = h


def _ffn(x, norm_g, wg, wu, wd, final_g, *, layer, final_norm, tm=1024, tf=256):
    T, D = x.shape
    Fd = wg.shape[-1]
    assert T % tm == 0 and Fd % tf == 0
    return pl.pallas_call(
        functools.partial(_ffn_kernel, final_norm=final_norm),
        grid=(T // tm, Fd // tf),
        in_specs=[
            pl.BlockSpec((tm, D), lambda i, j: (i, 0)),
            pl.BlockSpec((1, D), lambda i, j: (0, 0)),
            pl.BlockSpec((None, D, tf), lambda i, j: (layer, 0, j)),
            pl.BlockSpec((None, D, tf), lambda i, j: (layer, 0, j)),
            pl.BlockSpec((None, tf, D), lambda i, j: (layer, j, 0)),
            pl.BlockSpec((1, D), lambda i, j: (0, 0)),
        ],
        out_specs=pl.BlockSpec((tm, D), lambda i, j: (i, 0)),
        out_shape=jax.ShapeDtypeStruct((T, D), F32),
        scratch_shapes=[pltpu.VMEM((tm, D), BF16)],
        compiler_params=_cparams(("parallel", "arbitrary")),
        name="ffn",
    )(x, norm_g.reshape(1, D), wg, wu, wd, final_g.reshape(1, D))


def _proj_in_kernel(h_ref, g_ref, w_ref, o_ref):
    o_ref[...] = _dot(_rms(h_ref[...], g_ref[...]).astype(BF16), w_ref[...].astype(BF16))


def _proj_in(h, norm_g, w, *, layer, tm=512, tn=1536):
    T, D = h.shape
    N = w.shape[-1]
    assert T % tm == 0 and N % tn == 0
    return pl.pallas_call(
        _proj_in_kernel,
        grid=(N // tn, T // tm),
        in_specs=[
            pl.BlockSpec((tm, D), lambda j, i: (i, 0)),
            pl.BlockSpec((1, D), lambda j, i: (0, 0)),
            pl.BlockSpec((None, D, tn), lambda j, i: (layer, 0, j)),
        ],
        out_specs=pl.BlockSpec((tm, tn), lambda j, i: (i, j)),
        out_shape=jax.ShapeDtypeStruct((T, N), F32),
        compiler_params=_cparams(("arbitrary", "arbitrary")),
        name="proj_in",
    )(h, norm_g.reshape(1, D), w)


def _proj_out_kernel(h_ref, mp_ref, ms_ref, w_ref, o_ref, *, n_prompt_tiles):
    i = pl.program_id(1)
    w = w_ref[...].astype(BF16)

    @pl.when(i < n_prompt_tiles)
    def _():
        o_ref[...] = h_ref[...] + _dot(mp_ref[...], w)

    @pl.when(i >= n_prompt_tiles)
    def _():
        o_ref[...] = h_ref[...] + _dot(ms_ref[...], w)


def _proj_out(h, mixed_p, mixed_s, w, *, layer, tm=1024, tn=1024):
    T, D = h.shape
    K = mixed_p.shape[1]
    npt = mixed_p.shape[0] // tm
    assert T % tm == 0 and D % tn == 0 and mixed_p.shape[0] % tm == 0 and mixed_s.shape[0] % tm == 0
    assert mixed_p.shape[0] + mixed_s.shape[0] == T
    return pl.pallas_call(
        functools.partial(_proj_out_kernel, n_prompt_tiles=npt),
        grid=(D // tn, T // tm),
        in_specs=[
            pl.BlockSpec((tm, tn), lambda j, i: (i, j)),
            pl.BlockSpec((tm, K), lambda j, i: (jnp.minimum(i, npt - 1), 0)),
            pl.BlockSpec((tm, K), lambda j, i: (jnp.maximum(i - npt, 0), 0)),
            pl.BlockSpec((None, K, tn), lambda j, i: (layer, 0, j)),
        ],
        out_specs=pl.BlockSpec((tm, tn), lambda j, i: (i, j)),
        out_shape=jax.ShapeDtypeStruct((T, D), F32),
        compiler_params=_cparams(("arbitrary", "arbitrary")),
        name="proj_out",
    )(h, mixed_p, mixed_s, w)


def _ssm_param_kernel(are_ref, aim_ref, ldt_ref, bre_ref, bim_ref,
                      abr_ref, abi_ref, bbr_ref, bbi_ref, ptr_ref, pti_ref, *, n_pow):
    ar, ai = are_ref[...], aim_ref[...]
    dt = jnp.exp(ldt_ref[...])
    mag = jnp.exp(ar * dt)
    abr, abi = mag * jnp.cos(ai * dt), mag * jnp.sin(ai * dt)
    den = ar * ar + ai * ai
    nr, ni = abr - 1.0, abi
    cre = (nr * ar + ni * ai) / den
    cim = (ni * ar - nr * ai) / den
    br, bi = bre_ref[...], bim_ref[...]
    bbr_ref[...] = cre * br - cim * bi
    bbi_ref[...] = cre * bi + cim * br
    abr_ref[...] = abr
    abi_ref[...] = abi
    ptr_ref[0:1, :] = abr
    pti_ref[0:1, :] = abi
    qr, qi = abr, abi
    n = 1
    while n < n_pow:
        tr, ti = ptr_ref[0:n, :], pti_ref[0:n, :]
        ptr_ref[n:2 * n, :] = tr * qr - ti * qi
        pti_ref[n:2 * n, :] = tr * qi + ti * qr
        qr, qi = qr * qr - qi * qi, 2.0 * qr * qi
        n *= 2


def _ssm_params(a_re, a_im, log_dt, b_re, b_im, *, n_pow):
    Lyr = a_re.shape[0]
    Q = SSM_LANES
    assert n_pow & (n_pow - 1) == 0
    are = a_re.reshape(Lyr, 1, Q)
    aim = a_im.reshape(Lyr, 1, Q)
    ldt = jnp.repeat(log_dt, SSM_STATE, axis=-1).reshape(Lyr, 1, Q)
    bre = b_re.reshape(Lyr, Q, SSM_IN).transpose(0, 2, 1)
    bim = b_im.reshape(Lyr, Q, SSM_IN).transpose(0, 2, 1)
    row = lambda r: pl.BlockSpec((None, r, Q), lambda l: (l, 0, 0))
    return pl.pallas_call(
        functools.partial(_ssm_param_kernel, n_pow=n_pow),
        grid=(Lyr,),
        in_specs=[row(1), row(1), row(1), row(SSM_IN), row(SSM_IN)],
        out_specs=[row(1), row(1), row(SSM_IN), row(SSM_IN), row(n_pow), row(n_pow)],
        out_shape=[jax.ShapeDtypeStruct((Lyr, 1, Q), F32), jax.ShapeDtypeStruct((Lyr, 1, Q), F32),
                   jax.ShapeDtypeStruct((Lyr, SSM_IN, Q), F32), jax.ShapeDtypeStruct((Lyr, SSM_IN, Q), F32),
                   jax.ShapeDtypeStruct((Lyr, n_pow, Q), F32), jax.ShapeDtypeStruct((Lyr, n_pow, Q), F32)],
        compiler_params=_cparams(("arbitrary",)),
        name="ssm_params",
    )(are, aim, ldt, bre, bim)


def _block_diag_b(bbt_re, bbt_im):
    gl = SSM_GROUPS // SSM_BLOCKS
    eye = jnp.eye(gl, dtype=F32)

    def one(bt):
        a = bt.reshape(SSM_IN, SSM_BLOCKS, gl, SSM_STATE).transpose(1, 2, 0, 3)
        return jnp.einsum("bghp,gk->bghkp", a, eye).reshape(SSM_BLOCKS, SSM_BLOCK_IN, SSM_BLOCK_LANES)

    return jnp.concatenate([one(bbt_re), one(bbt_im)], axis=-1).astype(BF16)


def _block_diag_c(c):
    gl = SSM_GROUPS // SSM_BLOCKS
    eye = jnp.eye(gl, dtype=F32)
    a = c.reshape(SSM_BLOCKS, gl, SSM_IN, SSM_STATE)
    return jnp.einsum("bghp,gk->bkpgh", a, eye).reshape(SSM_BLOCKS, SSM_BLOCK_LANES, SSM_BLOCK_IN).astype(BF16)


def _group_norm(o, g):
    return o * lax.rsqrt(jnp.mean(o * o, axis=-1, keepdims=True) + EPS) * g


def _conv_tail(y, lng, lnb, wpw):
    mu = jnp.mean(y, axis=-1, keepdims=True)
    yc = y - mu
    var = jnp.mean(yc * yc, axis=-1, keepdims=True)
    z = yc * lax.rsqrt(var + EPS) * lng + lnb
    return _dot(jax.nn.silu(z).astype(BF16), wpw)


def _ssm_tail(y, u, d, wglu, bglu):
    g = jax.nn.gelu(y + d * u)
    return g * jax.nn.sigmoid(_dot(g.astype(BF16), wglu) + bglu)


def _ssm_readout(xr_ref, xi_ref, cr_ref, ci_ref):
    ys = []
    for blk in range(SSM_BLOCKS):
        sl = slice(blk * SSM_BLOCK_LANES, (blk + 1) * SSM_BLOCK_LANES)
        ys.append(_dot(xr_ref[:, sl].astype(BF16), cr_ref[blk]) + _dot(xi_ref[:, sl].astype(BF16), ci_ref[blk]))
    return jnp.concatenate(ys, axis=-1)


def _ssm_drive(u, bblk_ref, xr_ref, xi_ref):
    ub = u.astype(BF16)
    for blk in range(SSM_BLOCKS):
        res = _dot(ub[:, blk * SSM_BLOCK_IN:(blk + 1) * SSM_BLOCK_IN], bblk_ref[blk])
        sl = slice(blk * SSM_BLOCK_LANES, (blk + 1) * SSM_BLOCK_LANES)
        xr_ref[:, sl] = res[:, :SSM_BLOCK_LANES]
        xi_ref[:, sl] = res[:, SSM_BLOCK_LANES:]


def _prompt_mixer_kernel(
        pp_ref, su_ref, sv_ref, ps_ref, ca_ref, cg_ref,
        poolw_ref, pscale_ref, sgw_ref, sgb_ref,
        abr_ref, abi_ref, ptr_ref, pti_ref, bblk_ref, cr_ref, ci_ref, ssmd_ref, wglu_ref, bglu_ref,
        convw_ref, convb_ref, lng_ref, lnb_ref, wpw_ref, ong_ref,
        mixed_ref, pool_out, conv_out, sre_out, sim_out,
        pext, cext, csh, u3, ups, xr, xi, carr, cari, st_r, st_i, y3,
        *, tt):
    i = pl.program_id(1)
    tc = tt // SUBLANES
    gw = GROUP_WIDTH
    tail = min(TAIL_ROWS, tt)

    @pl.when(i == 0)
    def _():
        pext[0:POOL_PAD, :] = jnp.zeros((POOL_PAD, gw), F32)
        cext[0:CONV_PAD, :] = jnp.zeros((CONV_PAD, gw), F32)
        cext[tt + CONV_PAD:tt + CONV_PAD + SUBLANES, :] = jnp.zeros((SUBLANES, gw), F32)
        st_r[...] = jnp.zeros_like(st_r)
        st_i[...] = jnp.zeros_like(st_i)

    pext[POOL_PAD:POOL_PAD + tt, :] = pp_ref[...]
    rb = 64

    def pool_body(b, c):
        base = pl.multiple_of(b * rb, rb)
        n = rb + POOL_PAD
        e = pext[pl.ds(base, n), :]
        d = e
        sums = []
        for lvl in range(len(POOL_WINDOWS)):
            sh = 1 << lvl
            d = d[:, (POOL_GROUP if lvl > 0 else 0):]
            d = d + pltpu.roll(d, sh, axis=0)
            sums.append(d[POOL_PAD:, :POOL_GROUP])
        s = jnp.concatenate(sums, axis=-1)
        pos = lax.broadcasted_iota(jnp.int32, (rb, POOL_GROUP), 0) + (i * tt + base)
        cnt = jnp.concatenate([jnp.minimum(w, pos + 1).astype(F32) for w in POOL_WINDOWS], axis=-1)
        ups[pl.ds(base, rb), :] = s / cnt - e[POOL_PAD:, :]
        return c

    lax.fori_loop(0, tt // rb, pool_body, 0)
    for ch in range(tt // tail):
        rows = slice(ch * tail, (ch + 1) * tail)
        z = ups[rows, :].astype(BF16)
        o = jnp.concatenate([_dot(z[:, g * POOL_GROUP:(g + 1) * POOL_GROUP], poolw_ref[g])
                             for g in range(len(POOL_WINDOWS))], axis=-1) * pscale_ref[...]
        mixed_ref[rows, 0:gw] = _group_norm(o, ong_ref[:, 0:gw]).astype(mixed_ref.dtype)
    pool_out[0] = pext[tt + POOL_PAD - POOL_BUF:tt + POOL_PAD, :]
    pext[0:POOL_PAD, :] = pext[tt:tt + POOL_PAD, :]

    for c in range(tt // SG_CHUNK):
        rows = slice(c * SG_CHUNK, (c + 1) * SG_CHUNK)
        parts = []
        for h in range(SG_HEADS):
            cols = slice(h * SG_CHUNK, (h + 1) * SG_CHUNK)
            mix = _dot(sgw_ref[h], sv_ref[rows, cols].astype(BF16)) + sgb_ref[h]
            parts.append(su_ref[rows, cols] * mix)
        o = jnp.concatenate(parts, axis=-1)
        mixed_ref[rows, gw:2 * gw] = _group_norm(o, ong_ref[:, gw:2 * gw]).astype(mixed_ref.dtype)

    cext[CONV_PAD:CONV_PAD + tt, :] = ca_ref[...] * jax.nn.sigmoid(cg_ref[...])
    cb = 32
    off = CONV_PAD - CONV_BUF

    for r in range(1, SUBLANES):
        csh[r - 1] = cext[r:r + tt + CONV_PAD, :]

    def conv_body(b, c):
        base = pl.multiple_of(b * cb, cb)
        acc = jnp.zeros((cb, gw), F32) + convb_ref[...]
        for k in range(CONV_WIDTH):
            q, r = divmod(k + off, SUBLANES)
            rows = pl.ds(base + SUBLANES * q, cb)
            tap = cext[rows, :] if r == 0 else csh[r - 1, rows, :]
            acc = acc + convw_ref[k:k + 1, :] * tap
        ups[pl.ds(base, cb), :] = acc
        return c

    lax.fori_loop(0, tt // cb, conv_body, 0)
    for ch in range(tt // tail):
        rows = slice(ch * tail, (ch + 1) * tail)
        o = _conv_tail(ups[rows, :], lng_ref[...], lnb_ref[...], wpw_ref[...])
        mixed_ref[rows, 3 * gw:4 * gw] = _group_norm(o, ong_ref[:, 3 * gw:4 * gw]).astype(mixed_ref.dtype)
    conv_out[0] = cext[tt + CONV_PAD - CONV_BUF:tt + CONV_PAD, :]
    cext[0:CONV_PAD, :] = cext[tt:tt + CONV_PAD, :]

    u = ps_ref[...]
    nlb = gw // LANES
    for lb in range(nlb):
        u3[lb] = u[:, lb * LANES:(lb + 1) * LANES]
    for j in range(tc):
        ups[j * SUBLANES:(j + 1) * SUBLANES, :] = jnp.concatenate(
            [u3[lb, pl.ds(j, SUBLANES, stride=tc), :] for lb in range(nlb)], axis=-1)
    _ssm_drive(ups[...], bblk_ref, xr, xi)

    for lb in range(SSM_LANES // SCAN_LANES):
        ls = slice(lb * SCAN_LANES, (lb + 1) * SCAN_LANES)
        ar = jnp.broadcast_to(abr_ref[:, ls], (SUBLANES, SCAN_LANES))
        ai = jnp.broadcast_to(abi_ref[:, ls], (SUBLANES, SCAN_LANES))

        def scan_body(j, carry, ls=ls, ar=ar, ai=ai):
            sr, si = carry
            rows = pl.ds(pl.multiple_of(j * SUBLANES, SUBLANES), SUBLANES)
            nr = ar * sr - ai * si + xr[rows, ls]
            ni = ar * si + ai * sr + xi[rows, ls]
            xr[rows, ls] = nr
            xi[rows, ls] = ni
            return nr, ni

        zero = jnp.zeros((SUBLANES, SCAN_LANES), F32)
        lax.fori_loop(0, tc, scan_body, (zero, zero), unroll=4)

    car_r, car_i = st_r[...], st_i[...]
    at_r, at_i = ptr_ref[tc - 1:tc, :], pti_ref[tc - 1:tc, :]
    last = (tc - 1) * SUBLANES
    for c in range(SUBLANES):
        carr[c:c + 1, :] = car_r
        cari[c:c + 1, :] = car_i
        lf_r, lf_i = xr[last + c:last + c + 1, :], xi[last + c:last + c + 1, :]
        car_r, car_i = at_r * car_r - at_i * car_i + lf_r, at_r * car_i + at_i * car_r + lf_i
    st_r[...] = car_r
    st_i[...] = car_i
    sre_out[0] = car_r
    sim_out[0] = car_i

    for lb in range(SSM_LANES // SCAN_LANES):
        ls = slice(lb * SCAN_LANES, (lb + 1) * SCAN_LANES)
        c_r, c_i = carr[:, ls], cari[:, ls]

        def fix_body(jb, carry, ls=ls, c_r=c_r, c_i=c_i):
            j0 = pl.multiple_of(jb * SUBLANES, SUBLANES)
            tr, ti = ptr_ref[pl.ds(j0, SUBLANES), ls], pti_ref[pl.ds(j0, SUBLANES), ls]
            for jj in range(SUBLANES):
                rows = pl.ds(pl.multiple_of((j0 + jj) * SUBLANES, SUBLANES), SUBLANES)
                p_r, p_i = tr[jj:jj + 1, :], ti[jj:jj + 1, :]
                xr[rows, ls] = xr[rows, ls] + (p_r * c_r - p_i * c_i)
                xi[rows, ls] = xi[rows, ls] + (p_r * c_i + p_i * c_r)
            return carry

        lax.fori_loop(0, tc // SUBLANES, fix_body, 0)

    y = _ssm_readout(xr, xi, cr_ref, ci_ref)
    o = _ssm_tail(y, ups[...], ssmd_ref[...], wglu_ref[...], bglu_ref[...])
    on = _group_norm(o, ong_ref[:, 2 * gw:3 * gw])
    for j in range(tc):
        for lb in range(nlb):
            y3[lb, pl.ds(j, SUBLANES, stride=tc), :] = on[j * SUBLANES:(j + 1) * SUBLANES, lb * LANES:(lb + 1) * LANES]
    mixed_ref[:, 2 * gw:3 * gw] = jnp.concatenate([y3[lb] for lb in range(nlb)], axis=-1).astype(mixed_ref.dtype)


def _prompt_mixer(P, n_batch, seq, lw, *, tt):
    gw = GROUP_WIDTH
    nt = seq // tt
    tc = tt // SUBLANES
    assert seq % tt == 0 and tt % SG_CHUNK == 0 and tc % SUBLANES == 0 and lw["ptr"].shape[0] == tc
    col = lambda k: pl.BlockSpec((tt, gw), lambda b, i, k=k: (b * nt + i, k))
    full = lambda a: pl.BlockSpec(a.shape, lambda b, i, n=a.ndim: (0,) * n)
    weights = [lw[k] for k in ("poolw", "pscale", "sgw", "sgb", "abr", "abi", "ptr", "pti", "bblk", "cr", "ci",
                               "ssmd", "wglu", "bglu", "convw", "convb", "lng", "lnb", "wpw", "ong")]
    return pl.pallas_call(
        functools.partial(_prompt_mixer_kernel, tt=tt),
        grid=(n_batch, nt),
        in_specs=[col(k) for k in range(6)] + [full(w) for w in weights],
        out_specs=[
            pl.BlockSpec((tt, N_MIXERS * gw), lambda b, i: (b * nt + i, 0)),
            pl.BlockSpec((1, POOL_BUF, gw), lambda b, i: (b, 0, 0)),
            pl.BlockSpec((1, CONV_BUF, gw), lambda b, i: (b, 0, 0)),
            pl.BlockSpec((1, 1, SSM_LANES), lambda b, i: (b, 0, 0)),
            pl.BlockSpec((1, 1, SSM_LANES), lambda b, i: (b, 0, 0)),
        ],
        out_shape=[
            jax.ShapeDtypeStruct((n_batch * seq, N_MIXERS * gw), BF16),
            jax.ShapeDtypeStruct((n_batch, POOL_BUF, gw), F32),
            jax.ShapeDtypeStruct((n_batch, CONV_BUF, gw), F32),
            jax.ShapeDtypeStruct((n_batch, 1, SSM_LANES), F32),
            jax.ShapeDtypeStruct((n_batch, 1, SSM_LANES), F32),
        ],
        scratch_shapes=[
            pltpu.VMEM((tt + POOL_PAD, gw), F32),
            pltpu.VMEM((tt + CONV_PAD + SUBLANES, gw), F32),
            pltpu.VMEM((SUBLANES - 1, tt + CONV_PAD, gw), F32),
            pltpu.VMEM((gw // LANES, tt, LANES), F32),
            pltpu.VMEM((tt, gw), F32),
            pltpu.VMEM((tt, SSM_LANES), F32),
            pltpu.VMEM((tt, SSM_LANES), F32),
            pltpu.VMEM((SUBLANES, SSM_LANES), F32),
            pltpu.VMEM((SUBLANES, SSM_LANES), F32),
            pltpu.VMEM((1, SSM_LANES), F32),
            pltpu.VMEM((1, SSM_LANES), F32),
            pltpu.VMEM((gw // LANES, tt, LANES), F32),
        ],
        compiler_params=_cparams(("parallel", "arbitrary")),
        name="prompt_mixer",
    )(P, P, P, P, P, P, *weights)


def _sample_mixer_kernel(
        pp_ref, su_ref, sv_ref, ps_ref, ca_ref, cg_ref,
        pbuf_ref, cbuf_ref, sre_ref, sim_ref,
        poolw_ref, pscale_ref, sgwv_ref, sgbv_ref,
        abr_ref, abi_ref, bblk_ref, cr_ref, ci_ref, ssmd_ref, wglu_ref, bglu_ref,
        convw_ref, convb_ref, lng_ref, lnb_ref, wpw_ref, ong_ref,
        mixed_ref, pool_out, conv_out, sre_out, sim_out,
        zs, hs, ys, us, xr, xi,
        *, start):
    steps, nb, gw = pp_ref.shape

    def put(dst_col, o):
        on = _group_norm(o, ong_ref[:, dst_col * gw:(dst_col + 1) * gw]).astype(mixed_ref.dtype)
        for t in range(steps):
            mixed_ref[t, :, dst_col * gw:(dst_col + 1) * gw] = on[t * nb:(t + 1) * nb, :]

    def pe(r):
        return pbuf_ref[r] if r < POOL_BUF else pp_ref[r - POOL_BUF]

    for t in range(steps):
        parts = []
        for g, w in enumerate(POOL_WINDOWS):
            cols = slice(g * POOL_GROUP, (g + 1) * POOL_GROUP)
            s = pe(POOL_BUF + t)[:, cols]
            for k in range(1, w):
                s = s + pe(POOL_BUF + t - k)[:, cols]
            parts.append(s / float(min(w, start + t + 1)))
        zs[t * nb:(t + 1) * nb, :] = jnp.concatenate(parts, axis=-1) - pp_ref[t]
    z = zs[...].astype(BF16)
    o = jnp.concatenate([_dot(z[:, g * POOL_GROUP:(g + 1) * POOL_GROUP], poolw_ref[g])
                         for g in range(len(POOL_WINDOWS))], axis=-1) * pscale_ref[...]
    put(0, o)
    for k in range(POOL_BUF):
        pool_out[k] = pe(steps + k)

    for t in range(steps):
        mix = jnp.zeros((nb, gw), F32) + sgbv_ref[t:t + 1, :]
        for s_ in range(t + 1):
            mix = mix + sgwv_ref[t, s_:s_ + 1, :] * sv_ref[s_]
        zs[t * nb:(t + 1) * nb, :] = su_ref[t] * mix
    put(1, zs[...])

    for t in range(steps):
        hs[t] = ca_ref[t] * jax.nn.sigmoid(cg_ref[t])

    def ce(r):
        return cbuf_ref[r] if r < CONV_BUF else hs[r - CONV_BUF]

    for t in range(steps):
        acc = jnp.zeros((nb, gw), F32) + convb_ref[...]
        for k in range(CONV_WIDTH):
            acc = acc + convw_ref[k:k + 1, :] * ce(t + k)
        ys[t * nb:(t + 1) * nb, :] = acc
    put(3, _conv_tail(ys[...], lng_ref[...], lnb_ref[...], wpw_ref[...]))
    for k in range(CONV_BUF):
        conv_out[k] = ce(steps + k)

    for t in range(steps):
        us[t * nb:(t + 1) * nb, :] = ps_ref[t]
    _ssm_drive(us[...], bblk_ref, xr, xi)
    lw = 256
    for lb in range(SSM_LANES // lw):
        ls = slice(lb * lw, (lb + 1) * lw)
        ar, ai = abr_ref[:, ls], abi_ref[:, ls]
        sr, si = sre_ref[:, ls], sim_ref[:, ls]
        for t in range(steps):
            rows = slice(t * nb, (t + 1) * nb)
            sr, si = ar * sr - ai * si + xr[rows, ls], ar * si + ai * sr + xi[rows, ls]
            xr[rows, ls] = sr
            xi[rows, ls] = si
        sre_out[:, ls] = sr
        sim_out[:, ls] = si
    y = _ssm_readout(xr, xi, cr_ref, ci_ref)
    put(2, _ssm_tail(y, us[...], ssmd_ref[...], wglu_ref[...], bglu_ref[...]))


def _sample_mixer(P3, row0, pbuf, cbuf, s_re, s_im, sw, *, start, nb=32):
    gw = GROUP_WIDTH
    steps = sw["sgwv"].shape[0]
    n_seq = P3.shape[1]
    assert n_seq % nb == 0 and row0 % steps == 0
    blk0 = row0 // steps
    col = lambda k: pl.BlockSpec((steps, nb, gw), lambda i, k=k: (blk0, i, k))
    full = lambda a: pl.BlockSpec(a.shape, lambda i, n=a.ndim: (0,) * n)
    slab = lambda r, c: pl.BlockSpec((r, nb, c), lambda i: (0, i, 0))
    rows = pl.BlockSpec((nb, SSM_LANES), lambda i: (i, 0))
    weights = [sw[k] for k in ("poolw", "pscale", "sgwv", "sgbv", "abr", "abi", "bblk", "cr", "ci",
                               "ssmd", "wglu", "bglu", "convw", "convb", "lng", "lnb", "wpw", "ong")]
    m = steps * nb
    return pl.pallas_call(
        functools.partial(_sample_mixer_kernel, start=start),
        grid=(n_seq // nb,),
        in_specs=[col(k) for k in range(6)] + [slab(POOL_BUF, gw), slab(CONV_BUF, gw), rows, rows]
        + [full(w) for w in weights],
        out_specs=[slab(steps, N_MIXERS * gw), slab(POOL_BUF, gw), slab(CONV_BUF, gw), rows, rows],
        out_shape=[
            jax.ShapeDtypeStruct((steps, n_seq, N_MIXERS * gw), BF16),
            jax.ShapeDtypeStruct((POOL_BUF, n_seq, gw), F32),
            jax.ShapeDtypeStruct((CONV_BUF, n_seq, gw), F32),
            jax.ShapeDtypeStruct((n_seq, SSM_LANES), F32),
            jax.ShapeDtypeStruct((n_seq, SSM_LANES), F32),
        ],
        scratch_shapes=[
            pltpu.VMEM((m, gw), F32),
            pltpu.VMEM((steps, nb, gw), F32),
            pltpu.VMEM((m, gw), F32),
            pltpu.VMEM((m, gw), F32),
            pltpu.VMEM((m, SSM_LANES), F32),
            pltpu.VMEM((m, SSM_LANES), F32),
        ],
        compiler_params=_cparams(("parallel",)),
        name="sample_mixer",
    )(P3, P3, P3, P3, P3, P3, pbuf, cbuf, s_re, s_im, *weights)


PROMPT_TILE = 512
PAST_LEN = 16384


def kernel(x_prompt, x_sample, state_pool, state_conv, state_ssm_re, state_ssm_im, ffn1_norm, ffn1_w_gate, ffn1_w_up, ffn1_w_down, mix_norm, w_in, pool_w, pool_scale, sg_w, sg_b, ssm_a_re, ssm_a_im, ssm_log_dt, ssm_b_re, ssm_b_im, ssm_c_re, ssm_c_im, ssm_d, ssm_w_glu, ssm_b_glu, conv_w, conv_b, conv_ln_g, conv_ln_b, conv_w_pw, out_norm_g, w_out, ffn2_norm, ffn2_w_gate, ffn2_w_up, ffn2_w_down, final_norm):
    return _forward(x_prompt, x_sample, state_pool, state_conv, state_ssm_re, state_ssm_im, ffn1_norm, ffn1_w_gate, ffn1_w_up, ffn1_w_down, mix_norm, w_in, pool_w, pool_scale, sg_w, sg_b, ssm_a_re, ssm_a_im, ssm_log_dt, ssm_b_re, ssm_b_im, ssm_c_re, ssm_c_im, ssm_d, ssm_w_glu, ssm_b_glu, conv_w, conv_b, conv_ln_g, conv_ln_b, conv_w_pw, out_norm_g, w_out, ffn2_norm, ffn2_w_gate, ffn2_w_up, ffn2_w_down, final_norm, prompt_tile=PROMPT_TILE, past_len=PAST_LEN)


def _forward(x_prompt, x_sample, state_pool, state_conv, state_ssm_re, state_ssm_im, ffn1_norm, ffn1_w_gate, ffn1_w_up, ffn1_w_down, mix_norm, w_in, pool_w, pool_scale, sg_w, sg_b, ssm_a_re, ssm_a_im, ssm_log_dt, ssm_b_re, ssm_b_im, ssm_c_re, ssm_c_im, ssm_d, ssm_w_glu, ssm_b_glu, conv_w, conv_b, conv_ln_g, conv_ln_b, conv_w_pw, out_norm_g, w_out, ffn2_norm, ffn2_w_gate, ffn2_w_up, ffn2_w_down, final_norm, *, prompt_tile, past_len):
    n_batch, seq, d_model = x_prompt.shape
    n_seq, steps, _ = x_sample.shape
    depth = w_in.shape[0]
    n_p = n_batch * seq
    tm = math.gcd(math.gcd(n_p, steps * n_seq), ROW_TILE)
    gw = GROUP_WIDTH
    bf = lambda w: w.astype(BF16)
    row = lambda v: v.reshape(1, -1)

    x = jnp.concatenate([x_prompt.reshape(n_p, d_model),
                         x_sample.transpose(1, 0, 2).reshape(steps * n_seq, d_model)], axis=0)

    abr, abi, bbr, bbi, ptr, pti = _ssm_params(ssm_a_re, ssm_a_im, ssm_log_dt, ssm_b_re, ssm_b_im,
                                               n_pow=prompt_tile // SUBLANES)
    tril = jnp.tril(jnp.ones((SG_CHUNK, SG_CHUNK), dtype=bool))

    outs = {k: [] for k in ("pool_p", "pool_s", "conv_p", "conv_s", "re_p", "im_p", "re_s", "im_s", "v_s")}
    h = x
    for l in range(depth):
        sgw = jnp.where(tril[None], sg_w[l], 0.0)
        shared = dict(
            poolw=bf(pool_w[l]), pscale=row(pool_scale[l]), abr=abr[l], abi=abi[l],
            bblk=_block_diag_b(bbr[l], bbi[l]), cr=_block_diag_c(ssm_c_re[l]), ci=_block_diag_c(-ssm_c_im[l]),
            ssmd=row(ssm_d[l]), wglu=bf(ssm_w_glu[l]), bglu=row(ssm_b_glu[l]),
            convw=conv_w[l], convb=row(conv_b[l]), lng=row(conv_ln_g[l]), lnb=row(conv_ln_b[l]),
            wpw=bf(conv_w_pw[l]), ong=row(out_norm_g[l]))
        lw = dict(shared, sgw=bf(sgw), ptr=ptr[l], pti=pti[l],
                  sgb=jnp.broadcast_to(sg_b[l][:, :, None], (SG_HEADS, SG_CHUNK, SG_CHUNK)))
        sw = dict(shared,
                  sgwv=jnp.repeat(sgw[:, :steps, :steps].transpose(1, 2, 0), SG_CHUNK, axis=-1),
                  sgbv=jnp.repeat(sg_b[l][:, :steps].T, SG_CHUNK, axis=-1))

        h = _ffn(h, ffn1_norm[l], ffn1_w_gate, ffn1_w_up, ffn1_w_down, final_norm, layer=l, final_norm=False, tm=tm)
        proj = _proj_in(h, mix_norm[l], w_in, layer=l, tm=math.gcd(tm, 512))
        mixed_p, pool_p, conv_p, re_p, im_p = _prompt_mixer(proj, n_batch, seq, lw, tt=prompt_tile)
        mixed_s, pool_s, conv_s, re_s, im_s = _sample_mixer(
            proj.reshape(-1, n_seq, proj.shape[-1]), n_p // n_seq,
            state_pool[l].transpose(1, 0, 2), state_conv[l].transpose(1, 0, 2),
            state_ssm_re[l].reshape(n_seq, SSM_LANES), state_ssm_im[l].reshape(n_seq, SSM_LANES),
            sw, start=past_len)
        h = _proj_out(h, mixed_p, mixed_s.reshape(steps * n_seq, N_MIXERS * gw), w_out, layer=l, tm=tm)
        h = _ffn(h, ffn2_norm[l], ffn2_w_gate, ffn2_w_up, ffn2_w_down, final_norm, layer=l,
                 final_norm=(l == depth - 1), tm=tm)

        st = lambda a: a.reshape(-1, SSM_GROUPS, SSM_STATE)
        outs["pool_p"].append(pool_p)
        outs["conv_p"].append(conv_p)
        outs["re_p"].append(st(re_p))
        outs["im_p"].append(st(im_p))
        outs["pool_s"].append(pool_s.transpose(1, 0, 2))
        outs["conv_s"].append(conv_s.transpose(1, 0, 2))
        outs["re_s"].append(st(re_s))
        outs["im_s"].append(st(im_s))
        outs["v_s"].append(proj[n_p:, 2 * gw:3 * gw].reshape(steps, n_seq, gw).transpose(1, 0, 2))

    y_prompt = h[:n_p].reshape(n_batch, seq, d_model)
    y_sample = h[n_p:].reshape(steps, n_seq, d_model).transpose(1, 0, 2)
    stk = lambda k: jnp.stack(outs[k])
    return (y_prompt, y_sample, stk("pool_p"), stk("pool_s"), stk("conv_p"), stk("conv_s"),
            stk("re_p"), stk("im_p"), stk("re_s"), stk("im_s"), stk("v_s"))
```

```python
import functools
import math

import jax
import jax.numpy as jnp
from jax import lax
from jax.experimental import pallas as pl
from jax.experimental.pallas import tpu as pltpu

F32 = jnp.float32
BF16 = jnp.bfloat16

EPS = 1e-6
GROUP_WIDTH = 512
N_MIXERS = 4
POOL_WINDOWS = (2, 4, 8, 16)
POOL_GROUP = GROUP_WIDTH // len(POOL_WINDOWS)
POOL_BUF = max(POOL_WINDOWS) - 1
SG_CHUNK = 128
SG_HEADS = 4
SSM_IN = 16
SSM_GROUPS = GROUP_WIDTH // SSM_IN
SSM_STATE = 64
SSM_LANES = SSM_GROUPS * SSM_STATE
SSM_BLOCKS = 2
SSM_BLOCK_IN = GROUP_WIDTH // SSM_BLOCKS
SSM_BLOCK_LANES = SSM_LANES // SSM_BLOCKS
CONV_WIDTH = 31
CONV_BUF = CONV_WIDTH - 1

SUBLANES = 8
LANES = 128
VMEM_LIMIT_BYTES = 63 * 1024 * 1024

POOL_PAD = 16
CONV_PAD = 32
SCAN_LANES = 512
ROW_TILE = 1024
FFN_ROW_TILE = 768
FFN_COL_TILE = 512
TAIL_ROWS = 256


def _cparams(sem):
    return pltpu.CompilerParams(dimension_semantics=sem, vmem_limit_bytes=VMEM_LIMIT_BYTES)


def _rms(x, g):
    return x * lax.rsqrt(jnp.mean(x * x, axis=-1, keepdims=True) + EPS) * g


def _dot(a, b):
    return jnp.dot(a, b, preferred_element_type=F32)


def _ffn_kernel(x_ref, g_ref, wg_ref, wu_ref, wd_ref, fg_ref, o_ref, xn_ref, *, final_norm):
    j = pl.program_id(1)

    @pl.when(j == 0)
    def _():
        xn_ref[...] = _rms(x_ref[...], g_ref[...]).astype(BF16)
        o_ref[...] = jnp.zeros_like(o_ref)

    xn = xn_ref[...]
    g = _dot(xn, wg_ref[...].astype(BF16))
    u = _dot(xn, wu_ref[...].astype(BF16))
    a = (jax.nn.silu(g) * u).astype(BF16)
    o_ref[...] += _dot(a, wd_ref[...].astype(BF16))

    @pl.when(j == pl.num_programs(1) - 1)
    def _():
        h = x_ref[...] + 0.5 * o_ref[...]
        if final_norm:
            h = _rms(h, fg_ref[...])
        o_ref[...] = h


def _ffn(x, norm_g, wg, wu, wd, final_g, *, layer, final_norm, tm=1024, tf=256):
    T, D = x.shape
    Fd = wg.shape[-1]
    assert T % tm == 0 and Fd % tf == 0
    return pl.pallas_call(
        functools.partial(_ffn_kernel, final_norm=final_norm),
        grid=(T // tm, Fd // tf),
        in_specs=[
            pl.BlockSpec((tm, D), lambda i, j: (i, 0)),
            pl.BlockSpec((1, D), lambda i, j: (0, 0)),
            pl.BlockSpec((None, D, tf), lambda i, j: (layer, 0, j)),
            pl.BlockSpec((None, D, tf), lambda i, j: (layer, 0, j)),
            pl.BlockSpec((None, tf, D), lambda i, j: (layer, j, 0)),
            pl.BlockSpec((1, D), lambda i, j: (0, 0)),
        ],
        out_specs=pl.BlockSpec((tm, D), lambda i, j: (i, 0)),
        out_shape=jax.ShapeDtypeStruct((T, D), F32),
        scratch_shapes=[pltpu.VMEM((tm, D), BF16)],
        compiler_params=_cparams(("parallel", "arbitrary")),
        name="ffn",
    )(x, norm_g.reshape(1, D), wg, wu, wd, final_g.reshape(1, D))


def _proj_in_kernel(h_ref, g_ref, w_ref, o_ref):
    o_ref[...] = _dot(_rms(h_ref[...], g_ref[...]).astype(BF16), w_ref[...].astype(BF16))


def _proj_in(h, norm_g, w, *, layer, tm=512, tn=1536):
    T, D = h.shape
    N = w.shape[-1]
    assert T % tm == 0 and N % tn == 0
    return pl.pallas_call(
        _proj_in_kernel,
        grid=(N // tn, T // tm),
        in_specs=[
            pl.BlockSpec((tm, D), lambda j, i: (i, 0)),
            pl.BlockSpec((1, D), lambda j, i: (0, 0)),
            pl.BlockSpec((None, D, tn), lambda j, i: (layer, 0, j)),
        ],
        out_specs=pl.BlockSpec((tm, tn), lambda j, i: (i, j)),
        out_shape=jax.ShapeDtypeStruct((T, N), F32),
        compiler_params=_cparams(("arbitrary", "arbitrary")),
        name="proj_in",
    )(h, norm_g.reshape(1, D), w)


def _proj_out_kernel(h_ref, mp_ref, ms_ref, w_ref, o_ref, *, n_prompt_tiles):
    i = pl.program_id(1)
    w = w_ref[...].astype(BF16)

    @pl.when(i < n_prompt_tiles)
    def _():
        o_ref[...] = h_ref[...] + _dot(mp_ref[...], w)

    @pl.when(i >= n_prompt_tiles)
    def _():
        o_ref[...] = h_ref[...] + _dot(ms_ref[...], w)


def _proj_out(h, mixed_p, mixed_s, w, *, layer, tm=1024, tn=1024):
    T, D = h.shape
    K = mixed_p.shape[1]
    npt = mixed_p.shape[0] // tm
    assert T % tm == 0 and D % tn == 0 and mixed_p.shape[0] % tm == 0 and mixed_s.shape[0] % tm == 0
    assert mixed_p.shape[0] + mixed_s.shape[0] == T
    return pl.pallas_call(
        functools.partial(_proj_out_kernel, n_prompt_tiles=npt),
        grid=(D // tn, T // tm),
        in_specs=[
            pl.BlockSpec((tm, tn), lambda j, i: (i, j)),
            pl.BlockSpec((tm, K), lambda j, i: (jnp.minimum(i, npt - 1), 0)),
            pl.BlockSpec((tm, K), lambda j, i: (jnp.maximum(i - npt, 0), 0)),
            pl.BlockSpec((None, K, tn), lambda j, i: (layer, 0, j)),
        ],
        out_specs=pl.BlockSpec((tm, tn), lambda j, i: (i, j)),
        out_shape=jax.ShapeDtypeStruct((T, D), F32),
        compiler_params=_cparams(("arbitrary", "arbitrary")),
        name="proj_out",
    )(h, mixed_p, mixed_s, w)


def _ssm_param_kernel(are_ref, aim_ref, ldt_ref, bre_ref, bim_ref,
                      abr_ref, abi_ref, bbr_ref, bbi_ref, ptr_ref, pti_ref, *, n_pow):
    ar, ai = are_ref[...], aim_ref[...]
    dt = jnp.exp(ldt_ref[...])
    mag = jnp.exp(ar * dt)
    abr, abi = mag * jnp.cos(ai * dt), mag * jnp.sin(ai * dt)
    den = ar * ar + ai * ai
    nr, ni = abr - 1.0, abi
    cre = (nr * ar + ni * ai) / den
    cim = (ni * ar - nr * ai) / den
    br, bi = bre_ref[...], bim_ref[...]
    bbr_ref[...] = cre * br - cim * bi
    bbi_ref[...] = cre * bi + cim * br
    abr_ref[...] = abr
    abi_ref[...] = abi
    ptr_ref[0:1, :] = abr
    pti_ref[0:1, :] = abi
    qr, qi = abr, abi
    n = 1
    while n < n_pow:
        tr, ti = ptr_ref[0:n, :], pti_ref[0:n, :]
        ptr_ref[n:2 * n, :] = tr * qr - ti * qi
        pti_ref[n:2 * n, :] = tr * qi + ti * qr
        qr, qi = qr * qr - qi * qi, 2.0 * qr * qi
        n *= 2


def _ssm_params(a_re, a_im, log_dt, b_re, b_im, *, n_pow):
    Lyr = a_re.shape[0]
    Q = SSM_LANES
    assert n_pow & (n_pow - 1) == 0
    are = a_re.reshape(Lyr, 1, Q)
    aim = a_im.reshape(Lyr, 1, Q)
    ldt = jnp.repeat(log_dt, SSM_STATE, axis=-1).reshape(Lyr, 1, Q)
    bre = b_re.reshape(Lyr, Q, SSM_IN).transpose(0, 2, 1)
    bim = b_im.reshape(Lyr, Q, SSM_IN).transpose(0, 2, 1)
    row = lambda r: pl.BlockSpec((None, r, Q), lambda l: (l, 0, 0))
    return pl.pallas_call(
        functools.partial(_ssm_param_kernel, n_pow=n_pow),
        grid=(Lyr,),
        in_specs=[row(1), row(1), row(1), row(SSM_IN), row(SSM_IN)],
        out_specs=[row(1), row(1), row(SSM_IN), row(SSM_IN), row(n_pow), row(n_pow)],
        out_shape=[jax.ShapeDtypeStruct((Lyr, 1, Q), F32), jax.ShapeDtypeStruct((Lyr, 1, Q), F32),
                   jax.ShapeDtypeStruct((Lyr, SSM_IN, Q), F32), jax.ShapeDtypeStruct((Lyr, SSM_IN, Q), F32),
                   jax.ShapeDtypeStruct((Lyr, n_pow, Q), F32), jax.ShapeDtypeStruct((Lyr, n_pow, Q), F32)],
        compiler_params=_cparams(("arbitrary",)),
        name="ssm_params",
    )(are, aim, ldt, bre, bim)


def _block_diag_b(bbt_re, bbt_im):
    gl = SSM_GROUPS // SSM_BLOCKS
    eye = jnp.eye(gl, dtype=F32)

    def one(bt):
        a = bt.reshape(SSM_IN, SSM_BLOCKS, gl, SSM_STATE).transpose(1, 2, 0, 3)
        return jnp.einsum("bghp,gk->bghkp", a, eye).reshape(SSM_BLOCKS, SSM_BLOCK_IN, SSM_BLOCK_LANES)

    return jnp.concatenate([one(bbt_re), one(bbt_im)], axis=-1).astype(BF16)


def _block_diag_c(c):
    gl = SSM_GROUPS // SSM_BLOCKS
    eye = jnp.eye(gl, dtype=F32)
    a = c.reshape(SSM_BLOCKS, gl, SSM_IN, SSM_STATE)
    return jnp.einsum("bghp,gk->bkpgh", a, eye).reshape(SSM_BLOCKS, SSM_BLOCK_LANES, SSM_BLOCK_IN).astype(BF16)


def _group_norm(o, g):
    return o * lax.rsqrt(jnp.mean(o * o, axis=-1, keepdims=True) + EPS) * g


def _conv_tail(y, lng, lnb, wpw):
    mu = jnp.mean(y, axis=-1, keepdims=True)
    yc = y - mu
    var = jnp.mean(yc * yc, axis=-1, keepdims=True)
    z = yc * lax.rsqrt(var + EPS) * lng + lnb
    return _dot(jax.nn.silu(z).astype(BF16), wpw)


def _ssm_tail(y, u, d, wglu, bglu):
    g = jax.nn.gelu(y + d * u)
    return g * jax.nn.sigmoid(_dot(g.astype(BF16), wglu) + bglu)


def _ssm_readout(xr_ref, xi_ref, cr_ref, ci_ref):
    ys = []
    for blk in range(SSM_BLOCKS):
        sl = slice(blk * SSM_BLOCK_LANES, (blk + 1) * SSM_BLOCK_LANES)
        ys.append(_dot(xr_ref[:, sl].astype(BF16), cr_ref[blk]) + _dot(xi_ref[:, sl].astype(BF16), ci_ref[blk]))
    return jnp.concatenate(ys, axis=-1)


def _ssm_drive(u, bblk_ref, xr_ref, xi_ref):
    ub = u.astype(BF16)
    for blk in range(SSM_BLOCKS):
        res = _dot(ub[:, blk * SSM_BLOCK_IN:(blk + 1) * SSM_BLOCK_IN], bblk_ref[blk])
        sl = slice(blk * SSM_BLOCK_LANES, (blk + 1) * SSM_BLOCK_LANES)
        xr_ref[:, sl] = res[:, :SSM_BLOCK_LANES]
        xi_ref[:, sl] = res[:, SSM_BLOCK_LANES:]


def _prompt_mixer_kernel(
        pp_ref, su_ref, sv_ref, ps_ref, ca_ref, cg_ref,
        poolw_ref, pscale_ref, sgw_ref, sgb_ref,
        abr_ref, abi_ref, ptr_ref, pti_ref, bblk_ref, cr_ref, ci_ref, ssmd_ref, wglu_ref, bglu_ref,
        convw_ref, convb_ref, lng_ref, lnb_ref, wpw_ref, ong_ref,
        mixed_ref, pool_out, conv_out, sre_out, sim_out,
        pext, cext, csh, u3, ups, xr, xi, carr, cari, st_r, st_i, y3,
        *, tt):
    i = pl.program_id(1)
    tc = tt // SUBLANES
    gw = GROUP_WIDTH
    tail = min(TAIL_ROWS, tt)

    @pl.when(i == 0)
    def _():
        pext[0:POOL_PAD, :] = jnp.zeros((POOL_PAD, gw), F32)
        cext[0:CONV_PAD, :] = jnp.zeros((CONV_PAD, gw), F32)
        cext[tt + CONV_PAD:tt + CONV_PAD + SUBLANES, :] = jnp.zeros((SUBLANES, gw), F32)
        st_r[...] = jnp.zeros_like(st_r)
        st_i[...] = jnp.zeros_like(st_i)

    pext[POOL_PAD:POOL_PAD + tt, :] = pp_ref[...]
    rb = 64

    def pool_body(b, c):
        base = pl.multiple_of(b * rb, rb)
        n = rb + POOL_PAD
        e = pext[pl.ds(base, n), :]
        d = e
        sums = []
        for lvl in range(len(POOL_WINDOWS)):
            sh = 1 << lvl
            d = d[:, (POOL_GROUP if lvl > 0 else 0):]
            d = d + pltpu.roll(d, sh, axis=0)
            sums.append(d[POOL_PAD:, :POOL_GROUP])
        s = jnp.concatenate(sums, axis=-1)
        pos = lax.broadcasted_iota(jnp.int32, (rb, POOL_GROUP), 0) + (i * tt + base)
        cnt = jnp.concatenate([jnp.minimum(w, pos + 1).astype(F32) for w in POOL_WINDOWS], axis=-1)
        ups[pl.ds(base, rb), :] = s / cnt - e[POOL_PAD:, :]
        return c

    lax.fori_loop(0, tt // rb, pool_body, 0)
    for ch in range(tt // tail):
        rows = slice(ch * tail, (ch + 1) * tail)
        z = ups[rows, :].astype(BF16)
        o = jnp.concatenate([_dot(z[:, g * POOL_GROUP:(g + 1) * POOL_GROUP], poolw_ref[g])
                             for g in range(len(POOL_WINDOWS))], axis=-1) * pscale_ref[...]
        mixed_ref[rows, 0:gw] = _group_norm(o, ong_ref[:, 0:gw]).astype(mixed_ref.dtype)
    pool_out[0] = pext[tt + POOL_PAD - POOL_BUF:tt + POOL_PAD, :]
    pext[0:POOL_PAD, :] = pext[tt:tt + POOL_PAD, :]

    for c in range(tt // SG_CHUNK):
        rows = slice(c * SG_CHUNK, (c + 1) * SG_CHUNK)
        parts = []
        for h in range(SG_HEADS):
            cols = slice(h * SG_CHUNK, (h + 1) * SG_CHUNK)
            mix = _dot(sgw_ref[h], sv_ref[rows, cols].astype(BF16)) + sgb_ref[h]
            parts.append(su_ref[rows, cols] * mix)
        o = jnp.concatenate(parts, axis=-1)
        mixed_ref[rows, gw:2 * gw] = _group_norm(o, ong_ref[:, gw:2 * gw]).astype(mixed_ref.dtype)

    cext[CONV_PAD:CONV_PAD + tt, :] = ca_ref[...] * jax.nn.sigmoid(cg_ref[...])
    cb = 32
    off = CONV_PAD - CONV_BUF

    for r in range(1, SUBLANES):
        csh[r - 1] = cext[r:r + tt + CONV_PAD, :]

    def conv_body(b, c):
        base = pl.multiple_of(b * cb, cb)
        acc = jnp.zeros((cb, gw), F32) + convb_ref[...]
        for k in range(CONV_WIDTH):
            q, r = divmod(k + off, SUBLANES)
            rows = pl.ds(base + SUBLANES * q, cb)
            tap = cext[rows, :] if r == 0 else csh[r - 1, rows, :]
            acc = acc + convw_ref[k:k + 1, :] * tap
        ups[pl.ds(base, cb), :] = acc
        return c

    lax.fori_loop(0, tt // cb, conv_body, 0)
    for ch in range(tt // tail):
        rows = slice(ch * tail, (ch + 1) * tail)
        o = _conv_tail(ups[rows, :], lng_ref[...], lnb_ref[...], wpw_ref[...])
        mixed_ref[rows, 3 * gw:4 * gw] = _group_norm(o, ong_ref[:, 3 * gw:4 * gw]).astype(mixed_ref.dtype)
    conv_out[0] = cext[tt + CONV_PAD - CONV_BUF:tt + CONV_PAD, :]
    cext[0:CONV_PAD, :] = cext[tt:tt + CONV_PAD, :]

    u = ps_ref[...]
    nlb = gw // LANES
    for lb in range(nlb):
        u3[lb] = u[:, lb * LANES:(lb + 1) * LANES]
    for j in range(tc):
        ups[j * SUBLANES:(j + 1) * SUBLANES, :] = jnp.concatenate(
            [u3[lb, pl.ds(j, SUBLANES, stride=tc), :] for lb in range(nlb)], axis=-1)
    _ssm_drive(ups[...], bblk_ref, xr, xi)

    for lb in range(SSM_LANES // SCAN_LANES):
        ls = slice(lb * SCAN_LANES, (lb + 1) * SCAN_LANES)
        ar = jnp.broadcast_to(abr_ref[:, ls], (SUBLANES, SCAN_LANES))
        ai = jnp.broadcast_to(abi_ref[:, ls], (SUBLANES, SCAN_LANES))

        def scan_body(j, carry, ls=ls, ar=ar, ai=ai):
            sr, si = carry
            rows = pl.ds(pl.multiple_of(j * SUBLANES, SUBLANES), SUBLANES)
            nr = ar * sr - ai * si + xr[rows, ls]
            ni = ar * si + ai * sr + xi[rows, ls]
            xr[rows, ls] = nr
            xi[rows, ls] = ni
            return nr, ni

        zero = jnp.zeros((SUBLANES, SCAN_LANES), F32)
        lax.fori_loop(0, tc, scan_body, (zero, zero), unroll=4)

    car_r, car_i = st_r[...], st_i[...]
    at_r, at_i = ptr_ref[tc - 1:tc, :], pti_ref[tc - 1:tc, :]
    last = (tc - 1) * SUBLANES
    for c in range(SUBLANES):
        carr[c:c + 1, :] = car_r
        cari[c:c + 1, :] = car_i
        lf_r, lf_i = xr[last + c:last + c + 1, :], xi[last + c:last + c + 1, :]
        car_r, car_i = at_r * car_r - at_i * car_i + lf_r, at_r * car_i + at_i * car_r + lf_i
    st_r[...] = car_r
    st_i[...] = car_i
    sre_out[0] = car_r
    sim_out[0] = car_i

    for lb in range(SSM_LANES // SCAN_LANES):
        ls = slice(lb * SCAN_LANES, (lb + 1) * SCAN_LANES)
        c_r, c_i = carr[:, ls], cari[:, ls]

        def fix_body(jb, carry, ls=ls, c_r=c_r, c_i=c_i):
            j0 = pl.multiple_of(jb * SUBLANES, SUBLANES)
            tr, ti = ptr_ref[pl.ds(j0, SUBLANES), ls], pti_ref[pl.ds(j0, SUBLANES), ls]
            for jj in range(SUBLANES):
                rows = pl.ds(pl.multiple_of((j0 + jj) * SUBLANES, SUBLANES), SUBLANES)
                p_r, p_i = tr[jj:jj + 1, :], ti[jj:jj + 1, :]
                xr[rows, ls] = xr[rows, ls] + (p_r * c_r - p_i * c_i)
                xi[rows, ls] = xi[rows, ls] + (p_r * c_i + p_i * c_r)
            return carry

        lax.fori_loop(0, tc // SUBLANES, fix_body, 0)

    y = _ssm_readout(xr, xi, cr_ref, ci_ref)
    o = _ssm_tail(y, ups[...], ssmd_ref[...], wglu_ref[...], bglu_ref[...])
    on = _group_norm(o, ong_ref[:, 2 * gw:3 * gw])
    for j in range(tc):
        for lb in range(nlb):
            y3[lb, pl.ds(j, SUBLANES, stride=tc), :] = on[j * SUBLANES:(j + 1) * SUBLANES, lb * LANES:(lb + 1) * LANES]
    mixed_ref[:, 2 * gw:3 * gw] = jnp.concatenate([y3[lb] for lb in range(nlb)], axis=-1).astype(mixed_ref.dtype)


def _prompt_mixer(P, n_batch, seq, lw, *, tt):
    gw = GROUP_WIDTH
    nt = seq // tt
    tc = tt // SUBLANES
    assert seq % tt == 0 and tt % SG_CHUNK == 0 and tc % SUBLANES == 0 and lw["ptr"].shape[0] == tc
    col = lambda k: pl.BlockSpec((tt, gw), lambda b, i, k=k: (b * nt + i, k))
    full = lambda a: pl.BlockSpec(a.shape, lambda b, i, n=a.ndim: (0,) * n)
    weights = [lw[k] for k in ("poolw", "pscale", "sgw", "sgb", "abr", "abi", "ptr", "pti", "bblk", "cr", "ci",
                               "ssmd", "wglu", "bglu", "convw", "convb", "lng", "lnb", "wpw", "ong")]
    return pl.pallas_call(
        functools.partial(_prompt_mixer_kernel, tt=tt),
        grid=(n_batch, nt),
        in_specs=[col(k) for k in range(6)] + [full(w) for w in weights],
        out_specs=[
            pl.BlockSpec((tt, N_MIXERS * gw), lambda b, i: (b * nt + i, 0)),
            pl.BlockSpec((1, POOL_BUF, gw), lambda b, i: (b, 0, 0)),
            pl.BlockSpec((1, CONV_BUF, gw), lambda b, i: (b, 0, 0)),
            pl.BlockSpec((1, 1, SSM_LANES), lambda b, i: (b, 0, 0)),
            pl.BlockSpec((1, 1, SSM_LANES), lambda b, i: (b, 0, 0)),
        ],
        out_shape=[
            jax.ShapeDtypeStruct((n_batch * seq, N_MIXERS * gw), BF16),
            jax.ShapeDtypeStruct((n_batch, POOL_BUF, gw), F32),
            jax.ShapeDtypeStruct((n_batch, CONV_BUF, gw), F32),
            jax.ShapeDtypeStruct((n_batch, 1, SSM_LANES), F32),
            jax.ShapeDtypeStruct((n_batch, 1, SSM_LANES), F32),
        ],
        scratch_shapes=[
            pltpu.VMEM((tt + POOL_PAD, gw), F32),
            pltpu.VMEM((tt + CONV_PAD + SUBLANES, gw), F32),
            pltpu.VMEM((SUBLANES - 1, tt + CONV_PAD, gw), F32),
            pltpu.VMEM((gw // LANES, tt, LANES), F32),
            pltpu.VMEM((tt, gw), F32),
            pltpu.VMEM((tt, SSM_LANES), F32),
            pltpu.VMEM((tt, SSM_LANES), F32),
            pltpu.VMEM((SUBLANES, SSM_LANES), F32),
            pltpu.VMEM((SUBLANES, SSM_LANES), F32),
            pltpu.VMEM((1, SSM_LANES), F32),
            pltpu.VMEM((1, SSM_LANES), F32),
            pltpu.VMEM((gw // LANES, tt, LANES), F32),
        ],
        compiler_params=_cparams(("parallel", "arbitrary")),
        name="prompt_mixer",
    )(P, P, P, P, P, P, *weights)


def _sample_mixer_kernel(
        pp_ref, su_ref, sv_ref, ps_ref, ca_ref, cg_ref,
        pbuf_ref, cbuf_ref, sre_ref, sim_ref,
        poolw_ref, pscale_ref, sgwv_ref, sgbv_ref,
        abr_ref, abi_ref, bblk_ref, cr_ref, ci_ref, ssmd_ref, wglu_ref, bglu_ref,
        convw_ref, convb_ref, lng_ref, lnb_ref, wpw_ref, ong_ref,
        mixed_ref, pool_out, conv_out, sre_out, sim_out,
        zs, hs, ys, us, xr, xi,
        *, start):
    steps, nb, gw = pp_ref.shape

    def put(dst_col, o):
        on = _group_norm(o, ong_ref[:, dst_col * gw:(dst_col + 1) * gw]).astype(mixed_ref.dtype)
        for t in range(steps):
            mixed_ref[t, :, dst_col * gw:(dst_col + 1) * gw] = on[t * nb:(t + 1) * nb, :]

    def pe(r):
        return pbuf_ref[r] if r < POOL_BUF else pp_ref[r - POOL_BUF]

    for t in range(steps):
        parts = []
        for g, w in enumerate(POOL_WINDOWS):
            cols = slice(g * POOL_GROUP, (g + 1) * POOL_GROUP)
            s = pe(POOL_BUF + t)[:, cols]
            for k in range(1, w):
                s = s + pe(POOL_BUF + t - k)[:, cols]
            parts.append(s / float(min(w, start + t + 1)))
        zs[t * nb:(t + 1) * nb, :] = jnp.concatenate(parts, axis=-1) - pp_ref[t]
    z = zs[...].astype(BF16)
    o = jnp.concatenate([_dot(z[:, g * POOL_GROUP:(g + 1) * POOL_GROUP], poolw_ref[g])
                         for g in range(len(POOL_WINDOWS))], axis=-1) * pscale_ref[...]
    put(0, o)
    for k in range(POOL_BUF):
        pool_out[k] = pe(steps + k)

    for t in range(steps):
        mix = jnp.zeros((nb, gw), F32) + sgbv_ref[t:t + 1, :]
        for s_ in range(t + 1):
            mix = mix + sgwv_ref[t, s_:s_ + 1, :] * sv_ref[s_]
        zs[t * nb:(t + 1) * nb, :] = su_ref[t] * mix
    put(1, zs[...])

    for t in range(steps):
        hs[t] = ca_ref[t] * jax.nn.sigmoid(cg_ref[t])

    def ce(r):
        return cbuf_ref[r] if r < CONV_BUF else hs[r - CONV_BUF]

    for t in range(steps):
        acc = jnp.zeros((nb, gw), F32) + convb_ref[...]
        for k in range(CONV_WIDTH):
            acc = acc + convw_ref[k:k + 1, :] * ce(t + k)
        ys[t * nb:(t + 1) * nb, :] = acc
    put(3, _conv_tail(ys[...], lng_ref[...], lnb_ref[...], wpw_ref[...]))
    for k in range(CONV_BUF):
        conv_out[k] = ce(steps + k)

    for t in range(steps):
        us[t * nb:(t + 1) * nb, :] = ps_ref[t]
    _ssm_drive(us[...], bblk_ref, xr, xi)
    lw = 256
    for lb in range(SSM_LANES // lw):
        ls = slice(lb * lw, (lb + 1) * lw)
        ar, ai = abr_ref[:, ls], abi_ref[:, ls]
        sr, si = sre_ref[:, ls], sim_ref[:, ls]
        for t in range(steps):
            rows = slice(t * nb, (t + 1) * nb)
            sr, si = ar * sr - ai * si + xr[rows, ls], ar * si + ai * sr + xi[rows, ls]
            xr[rows, ls] = sr
            xi[rows, ls] = si
        sre_out[:, ls] = sr
        sim_out[:, ls] = si
    y = _ssm_readout(xr, xi, cr_ref, ci_ref)
    put(2, _ssm_tail(y, us[...], ssmd_ref[...], wglu_ref[...], bglu_ref[...]))


def _sample_mixer(P3, row0, pbuf, cbuf, s_re, s_im, sw, *, start, nb=32):
    gw = GROUP_WIDTH
    steps = sw["sgwv"].shape[0]
    n_seq = P3.shape[1]
    assert n_seq % nb == 0 and row0 % steps == 0
    blk0 = row0 // steps
    col = lambda k: pl.BlockSpec((steps, nb, gw), lambda i, k=k: (blk0, i, k))
    full = lambda a: pl.BlockSpec(a.shape, lambda i, n=a.ndim: (0,) * n)
    slab = lambda r, c: pl.BlockSpec((r, nb, c), lambda i: (0, i, 0))
    rows = pl.BlockSpec((nb, SSM_LANES), lambda i: (i, 0))
    weights = [sw[k] for k in ("poolw", "pscale", "sgwv", "sgbv", "abr", "abi", "bblk", "cr", "ci",
                               "ssmd", "wglu", "bglu", "convw", "convb", "lng", "lnb", "wpw", "ong")]
    m = steps * nb
    return pl.pallas_call(
        functools.partial(_sample_mixer_kernel, start=start),
        grid=(n_seq // nb,),
        in_specs=[col(k) for k in range(6)] + [slab(POOL_BUF, gw), slab(CONV_BUF, gw), rows, rows]
        + [full(w) for w in weights],
        out_specs=[slab(steps, N_MIXERS * gw), slab(POOL_BUF, gw), slab(CONV_BUF, gw), rows, rows],
        out_shape=[
            jax.ShapeDtypeStruct((steps, n_seq, N_MIXERS * gw), BF16),
            jax.ShapeDtypeStruct((POOL_BUF, n_seq, gw), F32),
            jax.ShapeDtypeStruct((CONV_BUF, n_seq, gw), F32),
            jax.ShapeDtypeStruct((n_seq, SSM_LANES), F32),
            jax.ShapeDtypeStruct((n_seq, SSM_LANES), F32),
        ],
        scratch_shapes=[
            pltpu.VMEM((m, gw), F32),
            pltpu.VMEM((steps, nb, gw), F32),
            pltpu.VMEM((m, gw), F32),
            pltpu.VMEM((m, gw), F32),
            pltpu.VMEM((m, SSM_LANES), F32),
            pltpu.VMEM((m, SSM_LANES), F32),
        ],
        compiler_params=_cparams(("parallel",)),
        name="sample_mixer",
    )(P3, P3, P3, P3, P3, P3, pbuf, cbuf, s_re, s_im, *weights)


PROMPT_TILE = 512
PAST_LEN = 16384


def kernel(x_prompt, x_sample, state_pool, state_conv, state_ssm_re, state_ssm_im, ffn1_norm, ffn1_w_gate, ffn1_w_up, ffn1_w_down, mix_norm, w_in, pool_w, pool_scale, sg_w, sg_b, ssm_a_re, ssm_a_im, ssm_log_dt, ssm_b_re, ssm_b_im, ssm_c_re, ssm_c_im, ssm_d, ssm_w_glu, ssm_b_glu, conv_w, conv_b, conv_ln_g, conv_ln_b, conv_w_pw, out_norm_g, w_out, ffn2_norm, ffn2_w_gate, ffn2_w_up, ffn2_w_down, final_norm):
    return _forward(x_prompt, x_sample, state_pool, state_conv, state_ssm_re, state_ssm_im, ffn1_norm, ffn1_w_gate, ffn1_w_up, ffn1_w_down, mix_norm, w_in, pool_w, pool_scale, sg_w, sg_b, ssm_a_re, ssm_a_im, ssm_log_dt, ssm_b_re, ssm_b_im, ssm_c_re, ssm_c_im, ssm_d, ssm_w_glu, ssm_b_glu, conv_w, conv_b, conv_ln_g, conv_ln_b, conv_w_pw, out_norm_g, w_out, ffn2_norm, ffn2_w_gate, ffn2_w_up, ffn2_w_down, final_norm, prompt_tile=PROMPT_TILE, past_len=PAST_LEN)


def _forward(x_prompt, x_sample, state_pool, state_conv, state_ssm_re, state_ssm_im, ffn1_norm, ffn1_w_gate, ffn1_w_up, ffn1_w_down, mix_norm, w_in, pool_w, pool_scale, sg_w, sg_b, ssm_a_re, ssm_a_im, ssm_log_dt, ssm_b_re, ssm_b_im, ssm_c_re, ssm_c_im, ssm_d, ssm_w_glu, ssm_b_glu, conv_w, conv_b, conv_ln_g, conv_ln_b, conv_w_pw, out_norm_g, w_out, ffn2_norm, ffn2_w_gate, ffn2_w_up, ffn2_w_down, final_norm, *, prompt_tile, past_len):
    n_batch, seq, d_model = x_prompt.shape
    n_seq, steps, _ = x_sample.shape
    depth = w_in.shape[0]
    n_p = n_batch * seq
    tm = math.gcd(math.gcd(n_p, steps * n_seq), ROW_TILE)
    n_rows = n_p + steps * n_seq
    ffn_tm = FFN_ROW_TILE if n_rows % FFN_ROW_TILE == 0 else tm
    gw = GROUP_WIDTH
    bf = lambda w: w.astype(BF16)
    row = lambda v: v.reshape(1, -1)

    x = jnp.concatenate([x_prompt.reshape(n_p, d_model),
                         x_sample.transpose(1, 0, 2).reshape(steps * n_seq, d_model)], axis=0)

    abr, abi, bbr, bbi, ptr, pti = _ssm_params(ssm_a_re, ssm_a_im, ssm_log_dt, ssm_b_re, ssm_b_im,
                                               n_pow=prompt_tile // SUBLANES)
    tril = jnp.tril(jnp.ones((SG_CHUNK, SG_CHUNK), dtype=bool))

    outs = {k: [] for k in ("pool_p", "pool_s", "conv_p", "conv_s", "re_p", "im_p", "re_s", "im_s", "v_s")}
    h = x
    for l in range(depth):
        sgw = jnp.where(tril[None], sg_w[l], 0.0)
        shared = dict(
            poolw=bf(pool_w[l]), pscale=row(pool_scale[l]), abr=abr[l], abi=abi[l],
            bblk=_block_diag_b(bbr[l], bbi[l]), cr=_block_diag_c(ssm_c_re[l]), ci=_block_diag_c(-ssm_c_im[l]),
            ssmd=row(ssm_d[l]), wglu=bf(ssm_w_glu[l]), bglu=row(ssm_b_glu[l]),
            convw=conv_w[l], convb=row(conv_b[l]), lng=row(conv_ln_g[l]), lnb=row(conv_ln_b[l]),
            wpw=bf(conv_w_pw[l]), ong=row(out_norm_g[l]))
        lw = dict(shared, sgw=bf(sgw), ptr=ptr[l], pti=pti[l],
                  sgb=jnp.broadcast_to(sg_b[l][:, :, None], (SG_HEADS, SG_CHUNK, SG_CHUNK)))
        sw = dict(shared,
                  sgwv=jnp.repeat(sgw[:, :steps, :steps].transpose(1, 2, 0), SG_CHUNK, axis=-1),
                  sgbv=jnp.repeat(sg_b[l][:, :steps].T, SG_CHUNK, axis=-1))

        h = _ffn(h, ffn1_norm[l], ffn1_w_gate, ffn1_w_up, ffn1_w_down, final_norm, layer=l, final_norm=False, tm=ffn_tm, tf=FFN_COL_TILE)
        proj = _proj_in(h, mix_norm[l], w_in, layer=l, tm=math.gcd(tm, 512))
        mixed_p, pool_p, conv_p, re_p, im_p = _prompt_mixer(proj, n_batch, seq, lw, tt=prompt_tile)
        mixed_s, pool_s, conv_s, re_s, im_s = _sample_mixer(
            proj.reshape(-1, n_seq, proj.shape[-1]), n_p // n_seq,
            state_pool[l].transpose(1, 0, 2), state_conv[l].transpose(1, 0, 2),
            state_ssm_re[l].reshape(n_seq, SSM_LANES), state_ssm_im[l].reshape(n_seq, SSM_LANES),
            sw, start=past_len)
        h = _proj_out(h, mixed_p, mixed_s.reshape(steps * n_seq, N_MIXERS * gw), w_out, layer=l, tm=tm)
        h = _ffn(h, ffn2_norm[l], ffn2_w_gate, ffn2_w_up, ffn2_w_down, final_norm, layer=l,
                 final_norm=(l == depth - 1), tm=ffn_tm, tf=FFN_COL_TILE)

        st = lambda a: a.reshape(-1, SSM_GROUPS, SSM_STATE)
        outs["pool_p"].append(pool_p)
        outs["conv_p"].append(conv_p)
        outs["re_p"].append(st(re_p))
        outs["im_p"].append(st(im_p))
        outs["pool_s"].append(pool_s.transpose(1, 0, 2))
        outs["conv_s"].append(conv_s.transpose(1, 0, 2))
        outs["re_s"].append(st(re_s))
        outs["im_s"].append(st(im_s))
        outs["v_s"].append(proj[n_p:, 2 * gw:3 * gw].reshape(steps, n_seq, gw).transpose(1, 0, 2))

    y_prompt = h[:n_p].reshape(n_batch, seq, d_model)
    y_sample = h[n_p:].reshape(steps, n_seq, d_model).transpose(1, 0, 2)
    stk = lambda k: jnp.stack(outs[k])
    return (y_prompt, y_sample, stk("pool_p"), stk("pool_s"), stk("conv_p"), stk("conv_s"),
            stk("re_p"), stk("im_p"), stk("re_s"), stk("im_s"), stk("v_s"))
```
